```python
import math
import jax, jax.numpy as jnp
from jax import lax
import numpy as np

D_MODEL = 1024
BATCH = 8
SEQ = 4096
DEPTH = 2

N_MIXERS = 2
D_FF = 2816
NORM_EPS = 1e-6

GLA_HEADS = 4
GLA_DK = D_MODEL // 2
GLA_DV = D_MODEL
GLA_HEAD_K = GLA_DK // GLA_HEADS
GLA_HEAD_V = GLA_DV // GLA_HEADS
GLA_GATE_RANK = 16
GLA_TAU = 16.0
GLA_CHUNK = 64
GLA_IN = 2 * GLA_DK + 2 * GLA_DV + GLA_GATE_RANK

DIFF_HEADS = 8
DIFF_HEAD_DIM = D_MODEL // (2 * DIFF_HEADS)
DIFF_V_DIM = 2 * DIFF_HEAD_DIM
DIFF_IN = 3 * D_MODEL
Q_BLOCK = 128

kernel_name = "hybrid_gla_diffattn_macaron"


def rms_norm(x, g):
    xf = x.astype(jnp.float32)
    y = xf * lax.rsqrt(jnp.mean(xf * xf, axis=-1, keepdims=True) + NORM_EPS)
    return (y * g.astype(jnp.float32)).astype(x.dtype)


def swiglu_ffn(h, w_in, w_down):
    gate, up = jnp.split(h @ w_in, 2, axis=-1)
    return (jax.nn.silu(gate) * up) @ w_down


def lambda_init_fn(layer_idx):
    return 0.8 - 0.6 * math.exp(-0.3 * layer_idx)


def gla_mixer(h, w_in, w_gate2, b_gate2, g_out, w_out):
    B, T, _ = h.shape
    n_chunks = T // GLA_CHUNK
    f32 = jnp.float32
    q, k, v, r, g_lr = jnp.split(h @ w_in, [GLA_DK, 2 * GLA_DK, 2 * GLA_DK + GLA_DV, 2 * GLA_DK + 2 * GLA_DV], axis=-1)
    log_a = jax.nn.log_sigmoid((g_lr @ w_gate2 + b_gate2).astype(f32)) / GLA_TAU

    def heads(t, d):
        return t.reshape(B, n_chunks, GLA_CHUNK, GLA_HEADS, d).transpose(0, 3, 1, 2, 4)

    q = heads(q, GLA_HEAD_K).astype(f32) * (GLA_HEAD_K ** -0.5)
    k = heads(k, GLA_HEAD_K).astype(f32)
    v = heads(v, GLA_HEAD_V).astype(f32)
    b = jnp.cumsum(heads(log_a, GLA_HEAD_K), axis=3)
    b_last = b[:, :, :, -1:, :]
    q_dec = q * jnp.exp(b)
    k_inv = k * jnp.exp(-b)
    k_tail = k * jnp.exp(b_last - b)

    causal = jnp.tril(jnp.ones((GLA_CHUNK, GLA_CHUNK), dtype=bool))
    att = jnp.where(causal, jnp.einsum('bhncd,bhnsd->bhncs', q_dec, k_inv), 0.0)
    o_intra = jnp.einsum('bhncs,bhnsv->bhncv', att, v)

    u = jnp.einsum('bhncd,bhncv->bhndv', k_tail, v)
    decay = jnp.exp(b_last[:, :, :, 0, :])

    def step(state, inp):
        dec, inc = inp
        return dec[..., None] * state + inc, state

    s0 = jnp.zeros((B, GLA_HEADS, GLA_HEAD_K, GLA_HEAD_V), f32)
    _, s_prev = lax.scan(step, s0, (jnp.moveaxis(decay, 2, 0), jnp.moveaxis(u, 2, 0)))
    s_prev = jnp.moveaxis(s_prev, 0, 2)
    o = o_intra + jnp.einsum('bhncd,bhndv->bhncv', q_dec, s_prev)

    o = rms_norm(o, g_out)
    o = o.transpose(0, 2, 3, 1, 4).reshape(B, T, GLA_DV).astype(h.dtype)
    return (o * jax.nn.silu(r)) @ w_out


def diff_mixer(h, w_in, lam_q1, lam_k1, lam_q2, lam_k2, g_out, w_out, lambda_init):
    B, T, _ = h.shape
    f32 = jnp.float32
    q, k, v = jnp.split(h @ w_in, 3, axis=-1)
    q = q.reshape(B, T, DIFF_HEADS, 2, DIFF_HEAD_DIM)
    k = k.reshape(B, T, DIFF_HEADS, 2, DIFF_HEAD_DIM)
    v = v.reshape(B, T, DIFF_HEADS, DIFF_V_DIM)
    lam = (jnp.exp(jnp.sum(lam_q1.astype(f32) * lam_k1.astype(f32)))
           - jnp.exp(jnp.sum(lam_q2.astype(f32) * lam_k2.astype(f32))) + lambda_init)
    scale = DIFF_HEAD_DIM ** -0.5

    outs = []
    for i in range(T // Q_BLOCK):
        q_end = (i + 1) * Q_BLOCK
        qb = q[:, i * Q_BLOCK:q_end]
        kb = k[:, :q_end]
        vb = v[:, :q_end]
        s = jnp.einsum('bqhcd,bkhcd->bhcqk', qb, kb).astype(f32) * scale
        q_pos = i * Q_BLOCK + jnp.arange(Q_BLOCK)
        k_pos = jnp.arange(q_end)
        s = jnp.where(q_pos[:, None] >= k_pos[None, :], s, -jnp.inf)
        p = jax.nn.softmax(s, axis=-1)
        pd = p[:, :, 0] - lam * p[:, :, 1]
        outs.append(jnp.einsum('bhqk,bkhv->bqhv', pd.astype(vb.dtype), vb))
    o = jnp.concatenate(outs, axis=1)

    o = rms_norm(o, g_out) * (1.0 - lambda_init)
    return o.reshape(B, T, D_MODEL) @ w_out


def setup_inputs(seed: int = 0) -> dict:
    key = jax.random.key(seed)
    ks = iter(jax.random.split(key, 64))

    def w(shape, fan_in):
        return jax.random.normal(next(ks), shape, jnp.float32) * (fan_in ** -0.5)

    def gain(n):
        return 1.0 + 0.02 * jax.random.normal(next(ks), (n,), jnp.float32)

    def small(shape, s):
        return s * jax.random.normal(next(ks), shape, jnp.float32)

    inp = {"x": jax.random.normal(next(ks), (BATCH, SEQ, D_MODEL), jnp.float32)}
    inp["l0_norm_ffn1"] = gain(D_MODEL)
    inp["l0_ffn1_w_in"] = w((D_MODEL, 2 * D_FF), D_MODEL)
    inp["l0_ffn1_w_down"] = w((D_FF, D_MODEL), D_FF)
    inp["l0_norm_mix"] = gain(D_MODEL)
    inp["l0_gla_w_in"] = w((D_MODEL, GLA_IN), D_MODEL)
    inp["l0_gla_w_gate2"] = w((GLA_GATE_RANK, GLA_DK), GLA_GATE_RANK)
    inp["l0_gla_b_gate2"] = small((GLA_DK,), 0.1)
    inp["l0_gla_norm_out"] = gain(GLA_HEAD_V)
    inp["l0_gla_w_out"] = w((GLA_DV, D_MODEL), GLA_DV)
    inp["l0_norm_ffn2"] = gain(D_MODEL)
    inp["l0_ffn2_w_in"] = w((D_MODEL, 2 * D_FF), D_MODEL)
    inp["l0_ffn2_w_down"] = w((D_FF, D_MODEL), D_FF)
    inp["l1_norm_ffn1"] = gain(D_MODEL)
    inp["l1_ffn1_w_in"] = w((D_MODEL, 2 * D_FF), D_MODEL)
    inp["l1_ffn1_w_down"] = w((D_FF, D_MODEL), D_FF)
    inp["l1_norm_mix"] = gain(D_MODEL)
    inp["l1_diff_w_in"] = w((D_MODEL, DIFF_IN), D_MODEL)
    inp["l1_diff_lambda_q1"] = small((DIFF_HEAD_DIM,), 0.1)
    inp["l1_diff_lambda_k1"] = small((DIFF_HEAD_DIM,), 0.1)
    inp["l1_diff_lambda_q2"] = small((DIFF_HEAD_DIM,), 0.1)
    inp["l1_diff_lambda_k2"] = small((DIFF_HEAD_DIM,), 0.1)
    inp["l1_diff_norm_out"] = gain(DIFF_V_DIM)
    inp["l1_diff_w_out"] = w((D_MODEL, D_MODEL), D_MODEL)
    inp["l1_norm_ffn2"] = gain(D_MODEL)
    inp["l1_ffn2_w_in"] = w((D_MODEL, 2 * D_FF), D_MODEL)
    inp["l1_ffn2_w_down"] = w((D_FF, D_MODEL), D_FF)
    inp["final_norm"] = gain(D_MODEL)
    return inp


def reference(x,
              l0_norm_ffn1, l0_ffn1_w_in, l0_ffn1_w_down,
              l0_norm_mix, l0_gla_w_in, l0_gla_w_gate2, l0_gla_b_gate2, l0_gla_norm_out, l0_gla_w_out,
              l0_norm_ffn2, l0_ffn2_w_in, l0_ffn2_w_down,
              l1_norm_ffn1, l1_ffn1_w_in, l1_ffn1_w_down,
              l1_norm_mix, l1_diff_w_in, l1_diff_lambda_q1, l1_diff_lambda_k1, l1_diff_lambda_q2,
              l1_diff_lambda_k2, l1_diff_norm_out, l1_diff_w_out,
              l1_norm_ffn2, l1_ffn2_w_in, l1_ffn2_w_down,
              final_norm):
    ffn_pre = [(l0_norm_ffn1, l0_ffn1_w_in, l0_ffn1_w_down),
               (l1_norm_ffn1, l1_ffn1_w_in, l1_ffn1_w_down)]
    ffn_post = [(l0_norm_ffn2, l0_ffn2_w_in, l0_ffn2_w_down),
                (l1_norm_ffn2, l1_ffn2_w_in, l1_ffn2_w_down)]
    mix_norms = [l0_norm_mix, l1_norm_mix]
    mixers = [
        lambda hh: gla_mixer(hh, l0_gla_w_in, l0_gla_w_gate2, l0_gla_b_gate2, l0_gla_norm_out, l0_gla_w_out),
        lambda hh: diff_mixer(hh, l1_diff_w_in, l1_diff_lambda_q1, l1_diff_lambda_k1, l1_diff_lambda_q2,
                              l1_diff_lambda_k2, l1_diff_norm_out, l1_diff_w_out, lambda_init_fn(1)),
    ]
    for i in range(DEPTH):
        g, w_in, w_down = ffn_pre[i]
        x = x + 0.5 * swiglu_ffn(rms_norm(x, g), w_in, w_down)
        x = x + mixers[i](rms_norm(x, mix_norms[i]))
        g, w_in, w_down = ffn_post[i]
        x = x + 0.5 * swiglu_ffn(rms_norm(x, g), w_in, w_down)
    return rms_norm(x, final_norm)
```

```python
import functools
import math

import jax
import jax.numpy as jnp
from jax import lax
from jax.experimental import pallas as pl
from jax.experimental.pallas import tpu as pltpu

F32 = jnp.float32
BF16 = jnp.bfloat16

D_MODEL = 1024
D_FF = 2816
NORM_EPS = 1e-6

GLA_HEADS = 4
GLA_DK = 512
GLA_DV = 1024
GLA_HEAD_K = 128
GLA_HEAD_V = 256
GLA_GATE_RANK = 16
GLA_TAU = 16.0
GLA_CHUNK = 64

DIFF_HEADS = 8
DIFF_HEAD_DIM = 64
DIFF_V_DIM = 128

LANES = 128
VMEM_LIMIT = 56 * 1024 * 1024

FFN_TM = 512
FFN_TF = 256
GLA_TM = 256
PROJ_TM = 512
ATT_TQ = 256
ATT_TK = 256

NT_DIMS = (((1,), (1,)), ((), ()))
TN_DIMS = (((0,), (0,)), ((), ()))


def _rms(x, g):
    ms = jnp.mean(x * x, axis=-1, keepdims=True)
    return (x * lax.rsqrt(ms + NORM_EPS)) * g


def _silu(x):
    return x * (1.0 / (1.0 + jnp.exp(-x)))


def _resident(shape):
    nd = len(shape)
    return pl.BlockSpec(shape, lambda *_: (0,) * nd, pipeline_mode=pl.Buffered(1))


def _params(sem):
    return pltpu.CompilerParams(dimension_semantics=sem, vmem_limit_bytes=VMEM_LIMIT)


def _ffn_body(x_ref, g_ref, win_ref, wdown_ref, gfin_ref, o_ref, *, final_norm):
    x = x_ref[...]
    xn = _rms(x, g_ref[...]).astype(BF16)
    acc = None
    for c in range(D_FF // FFN_TF):
        lo = c * FFN_TF
        gate = jnp.dot(xn, win_ref[:, lo:lo + FFN_TF], preferred_element_type=F32)
        up = jnp.dot(xn, win_ref[:, D_FF + lo:D_FF + lo + FFN_TF], preferred_element_type=F32)
        h = (_silu(gate) * up).astype(BF16)
        part = jnp.dot(h, wdown_ref[lo:lo + FFN_TF, :], preferred_element_type=F32)
        acc = part if acc is None else acc + part
    y = x + 0.5 * acc
    if final_norm:
        y = _rms(y, gfin_ref[...])
    o_ref[...] = y


def _ffn(x2d, g, w_in, w_down, g_final, final_norm):
    m = x2d.shape[0]
    row = pl.BlockSpec((FFN_TM, D_MODEL), lambda i: (i, 0))
    return pl.pallas_call(
        functools.partial(_ffn_body, final_norm=final_norm),
        grid=(m // FFN_TM,),
        in_specs=[row, _resident((1, D_MODEL)), _resident((D_MODEL, 2 * D_FF)),
                  _resident((D_FF, D_MODEL)), _resident((1, D_MODEL))],
        out_specs=row,
        out_shape=jax.ShapeDtypeStruct((m, D_MODEL), F32),
        compiler_params=_params(("arbitrary",)),
        name="ffn",
    )(x2d, g.reshape(1, -1), w_in.astype(BF16), w_down.astype(BF16), g_final.reshape(1, -1))


def _gla_body(x_ref, gmix_ref, win_ref, wg1_ref, wg2_ref, bg2_ref, gout_ref, wout_ref,
              o_ref, state_ref, og_ref):
    @pl.when(pl.program_id(1) == 0)
    def _():
        state_ref[...] = jnp.zeros_like(state_ref)

    x = x_ref[0]
    hn = _rms(x, gmix_ref[...]).astype(BF16)
    proj = jnp.dot(hn, win_ref[...], preferred_element_type=F32)
    g_lr = jnp.dot(hn, wg1_ref[...], preferred_element_type=F32)
    z = jnp.dot(g_lr.astype(BF16), wg2_ref[...], preferred_element_type=F32) + bg2_ref[...]
    log_a = (jnp.minimum(z, 0.0) - jnp.log(1.0 + jnp.exp(-jnp.abs(z)))) * (1.0 / GLA_TAU)

    rows = lax.broadcasted_iota(jnp.int32, (GLA_CHUNK, GLA_CHUNK), 0)
    cols = lax.broadcasted_iota(jnp.int32, (GLA_CHUNK, GLA_CHUNK), 1)
    causal = rows >= cols
    tril = causal.astype(F32)
    scale = GLA_HEAD_K ** -0.5

    for n in range(GLA_TM // GLA_CHUNK):
        r0 = n * GLA_CHUNK
        la = log_a[r0:r0 + GLA_CHUNK]
        b = jnp.dot(tril, la, preferred_element_type=F32, precision=lax.Precision.HIGHEST)
        b_last = b[GLA_CHUNK - 1:GLA_CHUNK]
        q = proj[r0:r0 + GLA_CHUNK, 0:GLA_DK]
        k = proj[r0:r0 + GLA_CHUNK, GLA_DK:2 * GLA_DK]
        q_dec = ((q * scale) * jnp.exp(b)).astype(BF16)
        k_inv = (k * jnp.exp(-b)).astype(BF16)
        k_tail = (k * jnp.exp(b_last - b)).astype(BF16)
        decay = jnp.exp(b_last)
        v = proj[r0:r0 + GLA_CHUNK, 2 * GLA_DK:2 * GLA_DK + GLA_DV].astype(BF16)
        r = proj[r0:r0 + GLA_CHUNK, 2 * GLA_DK + GLA_DV:]
        for h in range(GLA_HEADS):
            ks = slice(h * GLA_HEAD_K, (h + 1) * GLA_HEAD_K)
            vs = slice(h * GLA_HEAD_V, (h + 1) * GLA_HEAD_V)
            att = lax.dot_general(q_dec[:, ks], k_inv[:, ks], NT_DIMS, preferred_element_type=F32)
            att = jnp.where(causal, att, 0.0).astype(BF16)
            s_t = state_ref[h]
            o = jnp.dot(att, v[:, vs], preferred_element_type=F32)
            o = o + lax.dot_general(q_dec[:, ks], s_t.astype(BF16), NT_DIMS, preferred_element_type=F32)
            u_t = lax.dot_general(v[:, vs], k_tail[:, ks], TN_DIMS, preferred_element_type=F32)
            state_ref[h] = s_t * decay[:, ks] + u_t
            o = _rms(o, gout_ref[...])
            og_ref[r0:r0 + GLA_CHUNK, vs] = (o * _silu(r[:, vs])).astype(BF16)

    o_ref[0] = x + jnp.dot(og_ref[...], wout_ref[...], preferred_element_type=F32)


def _gla_layer(x, g_mix, w_in, w_gate2, b_gate2, g_out, w_out):
    bsz, t, _ = x.shape
    n_main = 2 * GLA_DK + 2 * GLA_DV
    w_main = w_in[:, :n_main].astype(BF16)
    w_g1 = jnp.pad(w_in[:, n_main:], ((0, 0), (0, LANES - GLA_GATE_RANK))).astype(BF16)
    w_g2 = jnp.pad(w_gate2, ((0, LANES - GLA_GATE_RANK), (0, 0))).astype(BF16)
    row = pl.BlockSpec((1, GLA_TM, D_MODEL), lambda b, i: (b, i, 0))
    return pl.pallas_call(
        _gla_body,
        grid=(bsz, t // GLA_TM),
        in_specs=[row, _resident((1, D_MODEL)), _resident((D_MODEL, n_main)),
                  _resident((D_MODEL, LANES)), _resident((LANES, GLA_DK)), _resident((1, GLA_DK)),
                  _resident((1, GLA_HEAD_V)), _resident((GLA_DV, D_MODEL))],
        out_specs=row,
        out_shape=jax.ShapeDtypeStruct(x.shape, F32),
        scratch_shapes=[pltpu.VMEM((GLA_HEADS, GLA_HEAD_V, GLA_HEAD_K), F32),
                        pltpu.VMEM((GLA_TM, GLA_DV), BF16)],
        compiler_params=_params(("arbitrary", "arbitrary")),
        name="gla_layer",
    )(x, g_mix.reshape(1, -1), w_main, w_g1, w_g2, b_gate2.reshape(1, -1),
      g_out.reshape(1, -1), w_out.astype(BF16))


def _qkv_body(x_ref, g_ref, w_ref, q_ref, k_ref, v_ref):
    hn = _rms(x_ref[...], g_ref[...]).astype(BF16)
    qkv = jnp.dot(hn, w_ref[...], preferred_element_type=F32)
    q_ref[...] = (qkv[:, :D_MODEL] * (DIFF_HEAD_DIM ** -0.5)).astype(BF16)
    k_ref[...] = qkv[:, D_MODEL:2 * D_MODEL].astype(BF16)
    v_ref[...] = qkv[:, 2 * D_MODEL:].astype(BF16)


def _qkv_proj(x2d, g, w_in):
    m = x2d.shape[0]
    row = pl.BlockSpec((PROJ_TM, D_MODEL), lambda i: (i, 0))
    out = jax.ShapeDtypeStruct((m, D_MODEL), BF16)
    return pl.pallas_call(
        _qkv_body,
        grid=(m // PROJ_TM,),
        in_specs=[row, _resident((1, D_MODEL)), _resident((D_MODEL, 3 * D_MODEL))],
        out_specs=[row, row, row],
        out_shape=[out, out, out],
        compiler_params=_params(("arbitrary",)),
        name="diff_qkv",
    )(x2d, g.reshape(1, -1), w_in.astype(BF16))


def _diff_attn_body(q_ref, k_ref, v_ref, lq1_ref, lk1_ref, lq2_ref, lk2_ref, gout_ref, o_ref,
                    *, lambda_init):
    i = pl.program_id(2)
    q = q_ref[0]
    lane = lax.broadcasted_iota(jnp.int32, q.shape, 1)
    zero = jnp.zeros_like(q)
    q2 = jnp.concatenate([jnp.where(lane < DIFF_HEAD_DIM, q, zero),
                          jnp.where(lane >= DIFF_HEAD_DIM, q, zero)], axis=0)
    m2 = 2 * ATT_TQ

    def step(j, carry, masked):
        m_prev, l_prev, acc = carry
        k = k_ref[0, pl.ds(j * ATT_TK, ATT_TK), :]
        v = v_ref[0, pl.ds(j * ATT_TK, ATT_TK), :]
        s = lax.dot_general(q2, k, NT_DIMS, preferred_element_type=F32)
        if masked:
            qpos = lax.broadcasted_iota(jnp.int32, s.shape, 0) % ATT_TQ
            kpos = lax.broadcasted_iota(jnp.int32, s.shape, 1)
            s = jnp.where(qpos >= kpos, s, -jnp.inf)
        m_new = jnp.maximum(m_prev, jnp.max(s, axis=-1, keepdims=True))
        alpha = jnp.exp(m_prev - m_new)
        p = jnp.exp(s - m_new)
        l_new = alpha * l_prev + jnp.sum(p, axis=-1, keepdims=True)
        acc = alpha * acc + jnp.dot(p.astype(BF16), v, preferred_element_type=F32)
        return m_new, l_new, acc

    init = (jnp.full((m2, 1), -jnp.inf, F32), jnp.zeros((m2, 1), F32), jnp.zeros((m2, DIFF_V_DIM), F32))
    carry = lax.fori_loop(0, i, functools.partial(step, masked=False), init)
    _, l_fin, acc = step(i, carry, masked=True)

    lam = (jnp.exp(jnp.sum(lq1_ref[...] * lk1_ref[...])) - jnp.exp(jnp.sum(lq2_ref[...] * lk2_ref[...]))
           + lambda_init)
    o = acc / l_fin
    o = o[:ATT_TQ] - lam * o[ATT_TQ:]
    o = _rms(o, gout_ref[...]) * (1.0 - lambda_init)
    o_ref[0] = o.astype(BF16)


def _diff_attn(q, k, v, lq1, lk1, lq2, lk2, g_out, lambda_init):
    bsz, t, _ = q.shape
    q_spec = pl.BlockSpec((1, ATT_TQ, DIFF_V_DIM), lambda b, h, i: (b, i, h))
    kv_spec = pl.BlockSpec((1, t, DIFF_V_DIM), lambda b, h, i: (b, 0, h))
    lam_spec = _resident((1, DIFF_HEAD_DIM))
    return pl.pallas_call(
        functools.partial(_diff_attn_body, lambda_init=lambda_init),
        grid=(bsz, DIFF_HEADS, t // ATT_TQ),
        in_specs=[q_spec, kv_spec, kv_spec, lam_spec, lam_spec, lam_spec, lam_spec,
                  _resident((1, DIFF_V_DIM))],
        out_specs=q_spec,
        out_shape=jax.ShapeDtypeStruct(q.shape, BF16),
        compiler_params=_params(("arbitrary", "arbitrary", "arbitrary")),
        name="diff_attn",
    )(q, k, v, lq1.reshape(1, -1), lk1.reshape(1, -1), lq2.reshape(1, -1), lk2.reshape(1, -1),
      g_out.reshape(1, -1))


def _out_proj_body(x_ref, o_ref, w_ref, y_ref):
    y_ref[...] = x_ref[...] + jnp.dot(o_ref[...], w_ref[...], preferred_element_type=F32)


def _out_proj(x2d, o2d, w_out):
    m = x2d.shape[0]
    row = pl.BlockSpec((PROJ_TM, D_MODEL), lambda i: (i, 0))
    return pl.pallas_call(
        _out_proj_body,
        grid=(m // PROJ_TM,),
        in_specs=[row, row, _resident((D_MODEL, D_MODEL))],
        out_specs=row,
        out_shape=jax.ShapeDtypeStruct((m, D_MODEL), F32),
        compiler_params=_params(("arbitrary",)),
        name="diff_out_proj",
    )(x2d, o2d, w_out.astype(BF16))


def kernel(x, l0_norm_ffn1, l0_ffn1_w_in, l0_ffn1_w_down, l0_norm_mix, l0_gla_w_in, l0_gla_w_gate2, l0_gla_b_gate2, l0_gla_norm_out, l0_gla_w_out, l0_norm_ffn2, l0_ffn2_w_in, l0_ffn2_w_down, l1_norm_ffn1, l1_ffn1_w_in, l1_ffn1_w_down, l1_norm_mix, l1_diff_w_in, l1_diff_lambda_q1, l1_diff_lambda_k1, l1_diff_lambda_q2, l1_diff_lambda_k2, l1_diff_norm_out, l1_diff_w_out, l1_norm_ffn2, l1_ffn2_w_in, l1_ffn2_w_down, final_norm):
    bsz, t, d = x.shape
    m = bsz * t
    lambda_init = 0.8 - 0.6 * math.exp(-0.3 * 1)

    h = _ffn(x.reshape(m, d), l0_norm_ffn1, l0_ffn1_w_in, l0_ffn1_w_down, final_norm, False)
    h = _gla_layer(h.reshape(bsz, t, d), l0_norm_mix, l0_gla_w_in, l0_gla_w_gate2, l0_gla_b_gate2,
                   l0_gla_norm_out, l0_gla_w_out)
    h = _ffn(h.reshape(m, d), l0_norm_ffn2, l0_ffn2_w_in, l0_ffn2_w_down, final_norm, False)

    h = _ffn(h, l1_norm_ffn1, l1_ffn1_w_in, l1_ffn1_w_down, final_norm, False)
    q, k, v = _qkv_proj(h, l1_norm_mix, l1_diff_w_in)
    o = _diff_attn(q.reshape(bsz, t, d), k.reshape(bsz, t, d), v.reshape(bsz, t, d),
                   l1_diff_lambda_q1, l1_diff_lambda_k1, l1_diff_lambda_q2, l1_diff_lambda_k2,
                   l1_diff_norm_out, lambda_init)
    h = _out_proj(h, o.reshape(m, d), l1_diff_w_out)
    h = _ffn(h, l1_norm_ffn2, l1_ffn2_w_in, l1_ffn2_w_down, final_norm, True)
    return h.reshape(bsz, t, d)
```

```python
import functools
import math

import jax
import jax.numpy as jnp
from jax import lax
from jax.experimental import pallas as pl
from jax.experimental.pallas import tpu as pltpu

F32 = jnp.float32
BF16 = jnp.bfloat16

D_MODEL = 1024
D_FF = 2816
NORM_EPS = 1e-6

GLA_HEADS = 4
GLA_DK = 512
GLA_DV = 1024
GLA_HEAD_K = 128
GLA_HEAD_V = 256
GLA_GATE_RANK = 16
GLA_TAU = 16.0
GLA_CHUNK = 64

DIFF_HEADS = 8
DIFF_HEAD_DIM = 64
DIFF_V_DIM = 128

LOG2_E = math.log2(math.e)
LANES = 128
BF16_ROWS = 16
VMEM_LIMIT = 56 * 1024 * 1024

FFN_TM = 512
FFN_TF = 256
GLA_TM = 256
PROJ_TM = 512
ATT_TQ = 512
ATT_TK = 512
ATT_G = 2

NT_DIMS = (((1,), (1,)), ((), ()))
TN_DIMS = (((0,), (0,)), ((), ()))


def _rms(x, g):
    ms = jnp.mean(x * x, axis=-1, keepdims=True)
    return (x * lax.rsqrt(ms + NORM_EPS)) * g


def _silu(x):
    return x * (1.0 / (1.0 + jnp.exp(-x)))


def _resident(shape):
    nd = len(shape)
    return pl.BlockSpec(shape, lambda *_: (0,) * nd, pipeline_mode=pl.Buffered(1))


def _params(sem):
    return pltpu.CompilerParams(dimension_semantics=sem, vmem_limit_bytes=VMEM_LIMIT)


def _ffn_body(x_ref, g_ref, win_ref, wdown_ref, gfin_ref, o_ref, *, final_norm):
    x = x_ref[...]
    xn = _rms(x, g_ref[...]).astype(BF16)
    acc = None
    for c in range(D_FF // FFN_TF):
        lo = c * FFN_TF
        gate = jnp.dot(xn, win_ref[:, lo:lo + FFN_TF], preferred_element_type=F32)
        up = jnp.dot(xn, win_ref[:, D_FF + lo:D_FF + lo + FFN_TF], preferred_element_type=F32)
        h = (_silu(gate) * up).astype(BF16)
        part = jnp.dot(h, wdown_ref[lo:lo + FFN_TF, :], preferred_element_type=F32)
        acc = part if acc is None else acc + part
    y = x + 0.5 * acc
    if final_norm:
        y = _rms(y, gfin_ref[...])
    o_ref[...] = y


def _ffn(x2d, g, w_in, w_down, g_final, final_norm):
    m = x2d.shape[0]
    row = pl.BlockSpec((FFN_TM, D_MODEL), lambda i: (i, 0))
    return pl.pallas_call(
        functools.partial(_ffn_body, final_norm=final_norm),
        grid=(m // FFN_TM,),
        in_specs=[row, _resident((1, D_MODEL)), _resident((D_MODEL, 2 * D_FF)),
                  _resident((D_FF, D_MODEL)), _resident((1, D_MODEL))],
        out_specs=row,
        out_shape=jax.ShapeDtypeStruct((m, D_MODEL), F32),
        compiler_params=_params(("arbitrary",)),
        name="ffn",
    )(x2d, g.reshape(1, -1), w_in.astype(BF16), w_down.astype(BF16), g_final.reshape(1, -1))


def _gla_body(x_ref, gmix_ref, win_ref, wg1_ref, wg2_ref, bg2_ref, gout_ref, wout_ref,
              o_ref, state_ref, og_ref):
    @pl.when(pl.program_id(1) == 0)
    def _():
        state_ref[...] = jnp.zeros_like(state_ref)

    x = x_ref[0]
    hn = _rms(x, gmix_ref[...]).astype(BF16)
    proj = jnp.dot(hn, win_ref[...], preferred_element_type=F32)
    g_lr = jnp.dot(hn, wg1_ref[...], preferred_element_type=F32)
    z = jnp.dot(g_lr.astype(BF16), wg2_ref[...], preferred_element_type=F32) + bg2_ref[...]
    log_a = (jnp.minimum(z, 0.0) - jnp.log(1.0 + jnp.exp(-jnp.abs(z)))) * (1.0 / GLA_TAU)

    rows = lax.broadcasted_iota(jnp.int32, (GLA_CHUNK, GLA_CHUNK), 0)
    cols = lax.broadcasted_iota(jnp.int32, (GLA_CHUNK, GLA_CHUNK), 1)
    causal = rows >= cols
    tril = causal.astype(F32)
    scale = GLA_HEAD_K ** -0.5

    for n in range(GLA_TM // GLA_CHUNK):
        r0 = n * GLA_CHUNK
        la = log_a[r0:r0 + GLA_CHUNK]
        b = jnp.dot(tril, la, preferred_element_type=F32, precision=lax.Precision.HIGHEST)
        b_last = b[GLA_CHUNK - 1:GLA_CHUNK]
        q = proj[r0:r0 + GLA_CHUNK, 0:GLA_DK]
        k = proj[r0:r0 + GLA_CHUNK, GLA_DK:2 * GLA_DK]
        q_dec = ((q * scale) * jnp.exp(b)).astype(BF16)
        k_inv = (k * jnp.exp(-b)).astype(BF16)
        k_tail = (k * jnp.exp(b_last - b)).astype(BF16)
        decay = jnp.exp(b_last)
        v = proj[r0:r0 + GLA_CHUNK, 2 * GLA_DK:2 * GLA_DK + GLA_DV].astype(BF16)
        r = proj[r0:r0 + GLA_CHUNK, 2 * GLA_DK + GLA_DV:]
        for h in range(GLA_HEADS):
            ks = slice(h * GLA_HEAD_K, (h + 1) * GLA_HEAD_K)
            vs = slice(h * GLA_HEAD_V, (h + 1) * GLA_HEAD_V)
            att = lax.dot_general(q_dec[:, ks], k_inv[:, ks], NT_DIMS, preferred_element_type=F32)
            att = jnp.where(causal, att, 0.0).astype(BF16)
            s_t = state_ref[h]
            o = jnp.dot(att, v[:, vs], preferred_element_type=F32)
            o = o + lax.dot_general(q_dec[:, ks], s_t.astype(BF16), NT_DIMS, preferred_element_type=F32)
            u_t = lax.dot_general(v[:, vs], k_tail[:, ks], TN_DIMS, preferred_element_type=F32)
            state_ref[h] = s_t * decay[:, ks] + u_t
            o = _rms(o, gout_ref[...])
            og_ref[r0:r0 + GLA_CHUNK, vs] = (o * _silu(r[:, vs])).astype(BF16)

    o_ref[0] = x + jnp.dot(og_ref[...], wout_ref[...], preferred_element_type=F32)


def _gla_layer(x, g_mix, w_in, w_gate2, b_gate2, g_out, w_out):
    bsz, t, _ = x.shape
    n_main = 2 * GLA_DK + 2 * GLA_DV
    w_main = w_in[:, :n_main].astype(BF16)
    w_g1 = jnp.pad(w_in[:, n_main:], ((0, 0), (0, LANES - GLA_GATE_RANK))).astype(BF16)
    w_g2 = jnp.pad(w_gate2, ((0, LANES - GLA_GATE_RANK), (0, 0))).astype(BF16)
    row = pl.BlockSpec((1, GLA_TM, D_MODEL), lambda b, i: (b, i, 0))
    return pl.pallas_call(
        _gla_body,
        grid=(bsz, t // GLA_TM),
        in_specs=[row, _resident((1, D_MODEL)), _resident((D_MODEL, n_main)),
                  _resident((D_MODEL, LANES)), _resident((LANES, GLA_DK)), _resident((1, GLA_DK)),
                  _resident((1, GLA_HEAD_V)), _resident((GLA_DV, D_MODEL))],
        out_specs=row,
        out_shape=jax.ShapeDtypeStruct(x.shape, F32),
        scratch_shapes=[pltpu.VMEM((GLA_HEADS, GLA_HEAD_V, GLA_HEAD_K), F32),
                        pltpu.VMEM((GLA_TM, GLA_DV), BF16)],
        compiler_params=_params(("arbitrary", "arbitrary")),
        name="gla_layer",
    )(x, g_mix.reshape(1, -1), w_main, w_g1, w_g2, b_gate2.reshape(1, -1),
      g_out.reshape(1, -1), w_out.astype(BF16))


def _qkv_body(x_ref, g_ref, wqt_ref, wk_ref, wvt_ref, qt_ref, k_ref, vt_ref):
    hn = _rms(x_ref[0], g_ref[...]).astype(BF16)
    qt = lax.dot_general(wqt_ref[...], hn, NT_DIMS, preferred_element_type=F32)
    qt_ref[0] = (qt * (DIFF_HEAD_DIM ** -0.5 * LOG2_E)).astype(BF16)
    k_ref[0] = jnp.dot(hn, wk_ref[...], preferred_element_type=F32).astype(BF16)
    vt = lax.dot_general(wvt_ref[...], hn, NT_DIMS, preferred_element_type=F32)
    for c in range(PROJ_TM // ATT_TK):
        vt_ref[0, c] = vt[:, c * ATT_TK:(c + 1) * ATT_TK].astype(BF16)


def _qkv_proj(x, g, w_in):
    bsz, t, _ = x.shape
    w_q, w_k, w_v = jnp.split(w_in.astype(BF16), 3, axis=1)
    return pl.pallas_call(
        _qkv_body,
        grid=(bsz, t // PROJ_TM),
        in_specs=[pl.BlockSpec((1, PROJ_TM, D_MODEL), lambda b, i: (b, i, 0)),
                  _resident((1, D_MODEL)), _resident((D_MODEL, D_MODEL)),
                  _resident((D_MODEL, D_MODEL)), _resident((D_MODEL, D_MODEL))],
        out_specs=[pl.BlockSpec((1, D_MODEL, PROJ_TM), lambda b, i: (b, 0, i)),
                   pl.BlockSpec((1, PROJ_TM, D_MODEL), lambda b, i: (b, i, 0)),
                   pl.BlockSpec((1, PROJ_TM // ATT_TK, D_MODEL, ATT_TK), lambda b, i: (b, i, 0, 0))],
        out_shape=[jax.ShapeDtypeStruct((bsz, D_MODEL, t), BF16),
                   jax.ShapeDtypeStruct((bsz, t, D_MODEL), BF16),
                   jax.ShapeDtypeStruct((bsz, t // ATT_TK, D_MODEL, ATT_TK), BF16)],
        compiler_params=_params(("arbitrary", "arbitrary")),
        name="diff_qkv",
    )(x, g.reshape(1, -1), w_q.T, w_k, w_v.T)


def _diff_attn_body(qt_ref, k_ref, vt_ref, lq1_ref, lk1_ref, lq2_ref, lk2_ref, gout_ref, o_ref,
                    *, lambda_init):
    i = pl.program_id(2)
    m2 = 2 * ATT_TQ
    feat = lax.broadcasted_iota(jnp.int32, (DIFF_V_DIM, ATT_TQ), 0)
    q2t = []
    for g in range(ATT_G):
        qt = qt_ref[0, g * DIFF_V_DIM:(g + 1) * DIFF_V_DIM, :].astype(F32)
        q2t.append(jnp.concatenate([jnp.where(feat < DIFF_HEAD_DIM, qt, 0.0),
                                    jnp.where(feat >= DIFF_HEAD_DIM, qt, 0.0)], axis=1).astype(BF16))

    def step(j, carry, masked):
        heads = [slice(g * DIFF_V_DIM, (g + 1) * DIFF_V_DIM) for g in range(ATT_G)]
        scores = []
        for g in range(ATT_G):
            k = k_ref[0, pl.ds(j * ATT_TK, ATT_TK), heads[g]]
            s = jnp.dot(k, q2t[g], preferred_element_type=F32)
            if masked:
                kpos = lax.broadcasted_iota(jnp.int32, s.shape, 0)
                qpos = lax.broadcasted_iota(jnp.int32, s.shape, 1) % ATT_TQ
                s = jnp.where(qpos >= kpos, s, -jnp.inf)
            scores.append(s)
        probs = []
        for g in range(ATT_G):
            m_prev = carry[g][0]
            m_new = jnp.maximum(m_prev, jnp.max(scores[g], axis=0, keepdims=True))
            alpha = jnp.exp2(m_prev - m_new)
            p = jnp.exp2(scores[g] - m_new)
            probs.append((m_new, alpha, p.astype(BF16)))
        out = []
        for g in range(ATT_G):
            m_new, alpha, p = probs[g]
            vt1 = jnp.concatenate([vt_ref[0, j, heads[g], :], ones_rows], axis=0)
            acc = alpha * carry[g][1] + jnp.dot(vt1, p, preferred_element_type=F32)
            out.append((m_new, acc))
        return tuple(out)

    ones_rows = jnp.ones((BF16_ROWS, ATT_TK), BF16)
    init = tuple((jnp.full((1, m2), -jnp.inf, F32), jnp.zeros((DIFF_V_DIM + BF16_ROWS, m2), F32))
                 for _ in range(ATT_G))
    carry = lax.fori_loop(0, i, functools.partial(step, masked=False), init)
    carry = step(i, carry, masked=True)

    lam = (jnp.exp(jnp.sum(lq1_ref[...] * lk1_ref[...])) - jnp.exp(jnp.sum(lq2_ref[...] * lk2_ref[...]))
           + lambda_init)
    for g in range(ATT_G):
        acc = carry[g][1]
        ot = acc[:DIFF_V_DIM] / acc[DIFF_V_DIM:DIFF_V_DIM + 1]
        o = (ot[:, :ATT_TQ] - lam * ot[:, ATT_TQ:]).T
        o = _rms(o, gout_ref[...]) * (1.0 - lambda_init)
        o_ref[0, :, g * DIFF_V_DIM:(g + 1) * DIFF_V_DIM] = o.astype(BF16)


def _diff_attn(qt, k, vt, lq1, lk1, lq2, lk2, g_out, lambda_init):
    bsz, t, _ = k.shape
    gw = ATT_G * DIFF_V_DIM
    lam_spec = _resident((1, DIFF_HEAD_DIM))
    return pl.pallas_call(
        functools.partial(_diff_attn_body, lambda_init=lambda_init),
        grid=(bsz, DIFF_HEADS // ATT_G, t // ATT_TQ),
        in_specs=[pl.BlockSpec((1, gw, ATT_TQ), lambda b, h, i: (b, h, i)),
                  pl.BlockSpec((1, t, gw), lambda b, h, i: (b, 0, h)),
                  pl.BlockSpec((1, t // ATT_TK, gw, ATT_TK), lambda b, h, i: (b, 0, h, 0)),
                  lam_spec, lam_spec, lam_spec, lam_spec, _resident((1, DIFF_V_DIM))],
        out_specs=pl.BlockSpec((1, ATT_TQ, gw), lambda b, h, i: (b, i, h)),
        out_shape=jax.ShapeDtypeStruct(k.shape, BF16),
        compiler_params=_params(("arbitrary", "arbitrary", "arbitrary")),
        name="diff_attn",
    )(qt, k, vt, lq1.reshape(1, -1), lk1.reshape(1, -1), lq2.reshape(1, -1), lk2.reshape(1, -1),
      g_out.reshape(1, -1))


def _out_proj_body(x_ref, o_ref, w_ref, y_ref):
    y_ref[...] = x_ref[...] + jnp.dot(o_ref[...], w_ref[...], preferred_element_type=F32)


def _out_proj(x2d, o2d, w_out):
    m = x2d.shape[0]
    row = pl.BlockSpec((PROJ_TM, D_MODEL), lambda i: (i, 0))
    return pl.pallas_call(
        _out_proj_body,
        grid=(m // PROJ_TM,),
        in_specs=[row, row, _resident((D_MODEL, D_MODEL))],
        out_specs=row,
        out_shape=jax.ShapeDtypeStruct((m, D_MODEL), F32),
        compiler_params=_params(("arbitrary",)),
        name="diff_out_proj",
    )(x2d, o2d, w_out.astype(BF16))


def kernel(x, l0_norm_ffn1, l0_ffn1_w_in, l0_ffn1_w_down, l0_norm_mix, l0_gla_w_in, l0_gla_w_gate2, l0_gla_b_gate2, l0_gla_norm_out, l0_gla_w_out, l0_norm_ffn2, l0_ffn2_w_in, l0_ffn2_w_down, l1_norm_ffn1, l1_ffn1_w_in, l1_ffn1_w_down, l1_norm_mix, l1_diff_w_in, l1_diff_lambda_q1, l1_diff_lambda_k1, l1_diff_lambda_q2, l1_diff_lambda_k2, l1_diff_norm_out, l1_diff_w_out, l1_norm_ffn2, l1_ffn2_w_in, l1_ffn2_w_down, final_norm):
    bsz, t, d = x.shape
    m = bsz * t
    lambda_init = 0.8 - 0.6 * math.exp(-0.3 * 1)

    h = _ffn(x.reshape(m, d), l0_norm_ffn1, l0_ffn1_w_in, l0_ffn1_w_down, final_norm, False)
    h = _gla_layer(h.reshape(bsz, t, d), l0_norm_mix, l0_gla_w_in, l0_gla_w_gate2, l0_gla_b_gate2,
                   l0_gla_norm_out, l0_gla_w_out)
    h = _ffn(h.reshape(m, d), l0_norm_ffn2, l0_ffn2_w_in, l0_ffn2_w_down, final_norm, False)

    h = _ffn(h, l1_norm_ffn1, l1_ffn1_w_in, l1_ffn1_w_down, final_norm, False)
    qt, k, vt = _qkv_proj(h.reshape(bsz, t, d), l1_norm_mix, l1_diff_w_in)
    o = _diff_attn(qt, k, vt, l1_diff_lambda_q1, l1_diff_lambda_k1, l1_diff_lambda_q2, l1_diff_lambda_k2,
                   l1_diff_norm_out, lambda_init)
    h = _out_proj(h, o.reshape(m, d), l1_diff_w_out)
    h = _ffn(h, l1_norm_ffn2, l1_ffn2_w_in, l1_ffn2_w_down, final_norm, True)
    return h.reshape(bsz, t, d)
```

```python
import functools
import math

import jax
import jax.numpy as jnp
from jax import lax
from jax.experimental import pallas as pl
from jax.experimental.pallas import tpu as pltpu

F32 = jnp.float32
BF16 = jnp.bfloat16

D_MODEL = 1024
D_FF = 2816
NORM_EPS = 1e-6

GLA_HEADS = 4
GLA_DK = 512
GLA_DV = 1024
GLA_HEAD_K = 128
GLA_HEAD_V = 256
GLA_GATE_RANK = 16
GLA_TAU = 16.0
GLA_CHUNK = 64
GLA_MAIN = 2 * GLA_DK + 2 * GLA_DV

DIFF_HEADS = 8
DIFF_HEAD_DIM = 64
DIFF_V_DIM = 128

LOG2_E = math.log2(math.e)
LANES = 128
BF16_ROWS = 16
VMEM_LIMIT = 56 * 1024 * 1024

FFN_TM = 512
FFN_TF = 256
GLA_PIECE = 512
GLA_TM = 512
PROJ_TM = 512
ATT_TQ = 512
ATT_TK = 512
ATT_G = 2

NT_DIMS = (((1,), (1,)), ((), ()))
TN_DIMS = (((0,), (0,)), ((), ()))


def _rms(x, g):
    ms = jnp.mean(x * x, axis=-1, keepdims=True)
    return (x * lax.rsqrt(ms + NORM_EPS)) * g


def _silu(x):
    return x * (1.0 / (1.0 + jnp.exp(-x)))


def _resident(shape):
    nd = len(shape)
    return pl.BlockSpec(shape, lambda *_: (0,) * nd, pipeline_mode=pl.Buffered(1))


def _params(sem):
    return pltpu.CompilerParams(dimension_semantics=sem, vmem_limit_bytes=VMEM_LIMIT)


def _ffn_body(x_ref, g_ref, win_ref, wdown_ref, gfin_ref, o_ref, *, final_norm):
    x = x_ref[...]
    xn = _rms(x, g_ref[...]).astype(BF16)
    acc = None
    for c in range(D_FF // FFN_TF):
        lo = c * FFN_TF
        gate = jnp.dot(xn, win_ref[:, lo:lo + FFN_TF], preferred_element_type=F32)
        up = jnp.dot(xn, win_ref[:, D_FF + lo:D_FF + lo + FFN_TF], preferred_element_type=F32)
        h = (_silu(gate) * up).astype(BF16)
        part = jnp.dot(h, wdown_ref[lo:lo + FFN_TF, :], preferred_element_type=F32)
        acc = part if acc is None else acc + part
    y = x + 0.5 * acc
    if final_norm:
        y = _rms(y, gfin_ref[...])
    o_ref[...] = y


def _ffn(x2d, g, w_in, w_down, g_final, final_norm):
    m = x2d.shape[0]
    row = pl.BlockSpec((FFN_TM, D_MODEL), lambda i: (i, 0))
    return pl.pallas_call(
        functools.partial(_ffn_body, final_norm=final_norm),
        grid=(m // FFN_TM,),
        in_specs=[row, _resident((1, D_MODEL)), _resident((D_MODEL, 2 * D_FF)),
                  _resident((D_FF, D_MODEL)), _resident((1, D_MODEL))],
        out_specs=row,
        out_shape=jax.ShapeDtypeStruct((m, D_MODEL), F32),
        compiler_params=_params(("arbitrary",)),
        name="ffn",
    )(x2d, g.reshape(1, -1), w_in.astype(BF16), w_down.astype(BF16), g_final.reshape(1, -1))


def _gla_body(xn_ref, xp_ref, gmix_ref, win_ref, wg1_ref, wg2_ref, bg2_ref, gout_ref, wout_ref,
              o_ref, proj_ref, la_ref, state_ref, og_ref, *, tiles_per_seq):
    g = pl.program_id(0)
    nxt = g % 2
    prv = 1 - nxt

    @pl.when(g == 0)
    def _():
        proj_ref[...] = jnp.zeros_like(proj_ref)
        la_ref[...] = jnp.zeros_like(la_ref)

    @pl.when((g == 0) | ((g - 1) % tiles_per_seq == 0))
    def _():
        state_ref[...] = jnp.zeros_like(state_ref)

    hn = _rms(xn_ref[...], gmix_ref[...]).astype(BF16)

    rows = lax.broadcasted_iota(jnp.int32, (GLA_CHUNK, GLA_CHUNK), 0)
    cols = lax.broadcasted_iota(jnp.int32, (GLA_CHUNK, GLA_CHUNK), 1)
    causal = rows >= cols
    tril = causal.astype(BF16)
    scale = GLA_HEAD_K ** -0.5
    n_chunks = GLA_TM // GLA_CHUNK
    chunk_heads = [(n, h) for n in range(n_chunks) for h in range(GLA_HEADS)]

    def project(piece):
        if piece < GLA_MAIN // GLA_PIECE:
            c0 = piece * GLA_PIECE
            proj_ref[nxt, :, c0:c0 + GLA_PIECE] = jnp.dot(hn, win_ref[:, c0:c0 + GLA_PIECE],
                                                          preferred_element_type=F32)
        else:
            g_lr = jnp.dot(hn, wg1_ref[...], preferred_element_type=F32)
            z = jnp.dot(g_lr.astype(BF16), wg2_ref[...], preferred_element_type=F32) + bg2_ref[...]
            la_ref[nxt] = (jnp.minimum(z, 0.0) - jnp.log(1.0 + jnp.exp(-jnp.abs(z)))) * (1.0 / GLA_TAU)

    def rows_of(n):
        return slice(n * GLA_CHUNK, (n + 1) * GLA_CHUNK)

    def ks(h):
        return slice(h * GLA_HEAD_K, (h + 1) * GLA_HEAD_K)

    def vs(h):
        return slice(h * GLA_HEAD_V, (h + 1) * GLA_HEAD_V)

    project(0)
    cum = []
    for n in range(n_chunks):
        la = la_ref[prv, rows_of(n), :]
        la_hi = la.astype(BF16)
        la_r = la - la_hi.astype(F32)
        la_mid = la_r.astype(BF16)
        la_lo = (la_r - la_mid.astype(F32)).astype(BF16)
        cum.append(jnp.dot(tril, la_hi, preferred_element_type=F32)
                   + jnp.dot(tril, la_mid, preferred_element_type=F32)
                   + jnp.dot(tril, la_lo, preferred_element_type=F32))

    project(1)
    q_dec, k_inv, k_tail, decay, v = [], [], [], [], []
    for n in range(n_chunks):
        b = cum[n]
        b_last = b[GLA_CHUNK - 1:GLA_CHUNK]
        q = proj_ref[prv, rows_of(n), 0:GLA_DK]
        k = proj_ref[prv, rows_of(n), GLA_DK:2 * GLA_DK]
        q_dec.append(((q * scale) * jnp.exp(b)).astype(BF16))
        k_inv.append((k * jnp.exp(-b)).astype(BF16))
        k_tail.append((k * jnp.exp(b_last - b)).astype(BF16))
        decay.append(jnp.exp(b_last))
        v.append(proj_ref[prv, rows_of(n), 2 * GLA_DK:2 * GLA_DK + GLA_DV].astype(BF16))

    project(2)
    att = {}
    for n, h in chunk_heads:
        a = lax.dot_general(q_dec[n][:, ks(h)], k_inv[n][:, ks(h)], NT_DIMS, preferred_element_type=F32)
        att[n, h] = jnp.where(causal, a, 0.0).astype(BF16)

    project(3)
    upd = {}
    for n, h in chunk_heads:
        upd[n, h] = lax.dot_general(v[n][:, vs(h)], k_tail[n][:, ks(h)], TN_DIMS,
                                    preferred_element_type=F32)

    project(4)
    start = {}
    for h in range(GLA_HEADS):
        s_t = state_ref[h]
        for n in range(n_chunks):
            start[n, h] = s_t.astype(BF16)
            s_t = s_t * decay[n][:, ks(h)] + upd[n, h]
        state_ref[h] = s_t

    project(5)
    outs = {}
    for n, h in chunk_heads:
        o = jnp.dot(att[n, h], v[n][:, vs(h)], preferred_element_type=F32)
        outs[n, h] = o + lax.dot_general(q_dec[n][:, ks(h)], start[n, h], NT_DIMS, preferred_element_type=F32)

    project(6)
    for n, h in chunk_heads:
        r = proj_ref[prv, rows_of(n), 2 * GLA_DK + GLA_DV + h * GLA_HEAD_V:2 * GLA_DK + GLA_DV + (h + 1) * GLA_HEAD_V]
        o = _rms(outs[n, h], gout_ref[...])
        og_ref[rows_of(n), vs(h)] = (o * _silu(r)).astype(BF16)

    o_ref[...] = xp_ref[...] + jnp.dot(og_ref[...], wout_ref[...], preferred_element_type=F32)


def _gla_layer(x2d, g_mix, w_in, w_gate2, b_gate2, g_out, w_out, seq_len):
    m = x2d.shape[0]
    n_tiles = m // GLA_TM
    w_main = w_in[:, :GLA_MAIN].astype(BF16)
    w_g1 = jnp.pad(w_in[:, GLA_MAIN:], ((0, 0), (0, LANES - GLA_GATE_RANK))).astype(BF16)
    w_g2 = jnp.pad(w_gate2, ((0, LANES - GLA_GATE_RANK), (0, 0))).astype(BF16)
    nxt_spec = pl.BlockSpec((GLA_TM, D_MODEL), lambda g: (jnp.minimum(g, n_tiles - 1), 0))
    prv_spec = pl.BlockSpec((GLA_TM, D_MODEL), lambda g: (jnp.maximum(g - 1, 0), 0))
    return pl.pallas_call(
        functools.partial(_gla_body, tiles_per_seq=seq_len // GLA_TM),
        grid=(n_tiles + 1,),
        in_specs=[nxt_spec, prv_spec, _resident((1, D_MODEL)), _resident((D_MODEL, GLA_MAIN)),
                  _resident((D_MODEL, LANES)), _resident((LANES, GLA_DK)), _resident((1, GLA_DK)),
                  _resident((1, GLA_HEAD_V)), _resident((GLA_DV, D_MODEL))],
        out_specs=prv_spec,
        out_shape=jax.ShapeDtypeStruct((m, D_MODEL), F32),
        scratch_shapes=[pltpu.VMEM((2, GLA_TM, GLA_MAIN), F32),
                        pltpu.VMEM((2, GLA_TM, GLA_DK), F32),
                        pltpu.VMEM((GLA_HEADS, GLA_HEAD_V, GLA_HEAD_K), F32),
                        pltpu.VMEM((GLA_TM, GLA_DV), BF16)],
        compiler_params=_params(("arbitrary",)),
        name="gla_layer",
    )(x2d, x2d, g_mix.reshape(1, -1), w_main, w_g1, w_g2, b_gate2.reshape(1, -1),
      g_out.reshape(1, -1), w_out.astype(BF16))


def _qkv_body(x_ref, g_ref, wqt_ref, wk_ref, wvt_ref, qt_ref, k_ref, vt_ref):
    hn = _rms(x_ref[0], g_ref[...]).astype(BF16)
    qt = lax.dot_general(wqt_ref[...], hn, NT_DIMS, preferred_element_type=F32)
    qt_ref[0] = (qt * (DIFF_HEAD_DIM ** -0.5 * LOG2_E)).astype(BF16)
    k_ref[0] = jnp.dot(hn, wk_ref[...], preferred_element_type=F32).astype(BF16)
    vt = lax.dot_general(wvt_ref[...], hn, NT_DIMS, preferred_element_type=F32)
    for c in range(PROJ_TM // ATT_TK):
        vt_ref[0, c] = vt[:, c * ATT_TK:(c + 1) * ATT_TK].astype(BF16)


def _qkv_proj(x, g, w_in):
    bsz, t, _ = x.shape
    w_q, w_k, w_v = jnp.split(w_in.astype(BF16), 3, axis=1)
    return pl.pallas_call(
        _qkv_body,
        grid=(bsz, t // PROJ_TM),
        in_specs=[pl.BlockSpec((1, PROJ_TM, D_MODEL), lambda b, i: (b, i, 0)),
                  _resident((1, D_MODEL)), _resident((D_MODEL, D_MODEL)),
                  _resident((D_MODEL, D_MODEL)), _resident((D_MODEL, D_MODEL))],
        out_specs=[pl.BlockSpec((1, D_MODEL, PROJ_TM), lambda b, i: (b, 0, i)),
                   pl.BlockSpec((1, PROJ_TM, D_MODEL), lambda b, i: (b, i, 0)),
                   pl.BlockSpec((1, PROJ_TM // ATT_TK, D_MODEL, ATT_TK), lambda b, i: (b, i, 0, 0))],
        out_shape=[jax.ShapeDtypeStruct((bsz, D_MODEL, t), BF16),
                   jax.ShapeDtypeStruct((bsz, t, D_MODEL), BF16),
                   jax.ShapeDtypeStruct((bsz, t // ATT_TK, D_MODEL, ATT_TK), BF16)],
        compiler_params=_params(("arbitrary", "arbitrary")),
        name="diff_qkv",
    )(x, g.reshape(1, -1), w_q.T, w_k, w_v.T)


def _diff_attn_body(qt_ref, k_ref, vt_ref, lq1_ref, lk1_ref, lq2_ref, lk2_ref, gout_ref, o_ref,
                    *, lambda_init):
    i = pl.program_id(2)
    m2 = 2 * ATT_TQ
    feat = lax.broadcasted_iota(jnp.int32, (DIFF_V_DIM, ATT_TQ), 0)
    q2t = []
    for g in range(ATT_G):
        qt = qt_ref[0, g * DIFF_V_DIM:(g + 1) * DIFF_V_DIM, :].astype(F32)
        q2t.append(jnp.concatenate([jnp.where(feat < DIFF_HEAD_DIM, qt, 0.0),
                                    jnp.where(feat >= DIFF_HEAD_DIM, qt, 0.0)], axis=1).astype(BF16))

    def step(j, carry, masked):
        heads = [slice(g * DIFF_V_DIM, (g + 1) * DIFF_V_DIM) for g in range(ATT_G)]
        scores = []
        for g in range(ATT_G):
            k = k_ref[0, pl.ds(j * ATT_TK, ATT_TK), heads[g]]
            s = jnp.dot(k, q2t[g], preferred_element_type=F32)
            if masked:
                kpos = lax.broadcasted_iota(jnp.int32, s.shape, 0)
                qpos = lax.broadcasted_iota(jnp.int32, s.shape, 1) % ATT_TQ
                s = jnp.where(qpos >= kpos, s, -jnp.inf)
            scores.append(s)
        probs = []
        for g in range(ATT_G):
            m_prev = carry[g][0]
            m_new = jnp.maximum(m_prev, jnp.max(scores[g], axis=0, keepdims=True))
            alpha = jnp.exp2(m_prev - m_new)
            p = jnp.exp2(scores[g] - m_new)
            probs.append((m_new, alpha, p.astype(BF16)))
        out = []
        for g in range(ATT_G):
            m_new, alpha, p = probs[g]
            vt1 = jnp.concatenate([vt_ref[0, j, heads[g], :], ones_rows], axis=0)
            acc = alpha * carry[g][1] + jnp.dot(vt1, p, preferred_element_type=F32)
            out.append((m_new, acc))
        return tuple(out)

    ones_rows = jnp.ones((BF16_ROWS, ATT_TK), BF16)
    init = tuple((jnp.full((1, m2), -jnp.inf, F32), jnp.zeros((DIFF_V_DIM + BF16_ROWS, m2), F32))
                 for _ in range(ATT_G))
    carry = lax.fori_loop(0, i, functools.partial(step, masked=False), init)
    carry = step(i, carry, masked=True)

    lam = (jnp.exp(jnp.sum(lq1_ref[...] * lk1_ref[...])) - jnp.exp(jnp.sum(lq2_ref[...] * lk2_ref[...]))
           + lambda_init)
    for g in range(ATT_G):
        acc = carry[g][1]
        ot = acc[:DIFF_V_DIM] / acc[DIFF_V_DIM:DIFF_V_DIM + 1]
        o = (ot[:, :ATT_TQ] - lam * ot[:, ATT_TQ:]).T
        o = _rms(o, gout_ref[...]) * (1.0 - lambda_init)
        o_ref[0, :, g * DIFF_V_DIM:(g + 1) * DIFF_V_DIM] = o.astype(BF16)


def _diff_attn(qt, k, vt, lq1, lk1, lq2, lk2, g_out, lambda_init):
    bsz, t, _ = k.shape
    gw = ATT_G * DIFF_V_DIM
    lam_spec = _resident((1, DIFF_HEAD_DIM))
    return pl.pallas_call(
        functools.partial(_diff_attn_body, lambda_init=lambda_init),
        grid=(bsz, DIFF_HEADS // ATT_G, t // ATT_TQ),
        in_specs=[pl.BlockSpec((1, gw, ATT_TQ), lambda b, h, i: (b, h, i)),
                  pl.BlockSpec((1, t, gw), lambda b, h, i: (b, 0, h)),
                  pl.BlockSpec((1, t // ATT_TK, gw, ATT_TK), lambda b, h, i: (b, 0, h, 0)),
                  lam_spec, lam_spec, lam_spec, lam_spec, _resident((1, DIFF_V_DIM))],
        out_specs=pl.BlockSpec((1, ATT_TQ, gw), lambda b, h, i: (b, i, h)),
        out_shape=jax.ShapeDtypeStruct(k.shape, BF16),
        compiler_params=_params(("arbitrary", "arbitrary", "arbitrary")),
        name="diff_attn",
    )(qt, k, vt, lq1.reshape(1, -1), lk1.reshape(1, -1), lq2.reshape(1, -1), lk2.reshape(1, -1),
      g_out.reshape(1, -1))


def _out_proj_body(x_ref, o_ref, w_ref, y_ref):
    y_ref[...] = x_ref[...] + jnp.dot(o_ref[...], w_ref[...], preferred_element_type=F32)


def _out_proj(x2d, o2d, w_out):
    m = x2d.shape[0]
    row = pl.BlockSpec((PROJ_TM, D_MODEL), lambda i: (i, 0))
    return pl.pallas_call(
        _out_proj_body,
        grid=(m // PROJ_TM,),
        in_specs=[row, row, _resident((D_MODEL, D_MODEL))],
        out_specs=row,
        out_shape=jax.ShapeDtypeStruct((m, D_MODEL), F32),
        compiler_params=_params(("arbitrary",)),
        name="diff_out_proj",
    )(x2d, o2d, w_out.astype(BF16))


def kernel(x, l0_norm_ffn1, l0_ffn1_w_in, l0_ffn1_w_down, l0_norm_mix, l0_gla_w_in, l0_gla_w_gate2, l0_gla_b_gate2, l0_gla_norm_out, l0_gla_w_out, l0_norm_ffn2, l0_ffn2_w_in, l0_ffn2_w_down, l1_norm_ffn1, l1_ffn1_w_in, l1_ffn1_w_down, l1_norm_mix, l1_diff_w_in, l1_diff_lambda_q1, l1_diff_lambda_k1, l1_diff_lambda_q2, l1_diff_lambda_k2, l1_diff_norm_out, l1_diff_w_out, l1_norm_ffn2, l1_ffn2_w_in, l1_ffn2_w_down, final_norm):
    bsz, t, d = x.shape
    m = bsz * t
    lambda_init = 0.8 - 0.6 * math.exp(-0.3 * 1)

    h = _ffn(x.reshape(m, d), l0_norm_ffn1, l0_ffn1_w_in, l0_ffn1_w_down, final_norm, False)
    h = _gla_layer(h, l0_norm_mix, l0_gla_w_in, l0_gla_w_gate2, l0_gla_b_gate2,
                   l0_gla_norm_out, l0_gla_w_out, t)
    h = _ffn(h, l0_norm_ffn2, l0_ffn2_w_in, l0_ffn2_w_down, final_norm, False)

    h = _ffn(h, l1_norm_ffn1, l1_ffn1_w_in, l1_ffn1_w_down, final_norm, False)
    qt, k, vt = _qkv_proj(h.reshape(bsz, t, d), l1_norm_mix, l1_diff_w_in)
    o = _diff_attn(qt, k, vt, l1_diff_lambda_q1, l1_diff_lambda_k1, l1_diff_lambda_q2, l1_diff_lambda_k2,
                   l1_diff_norm_out, lambda_init)
    h = _out_proj(h, o.reshape(m, d), l1_diff_w_out)
    h = _ffn(h, l1_norm_ffn2, l1_ffn2_w_in, l1_ffn2_w_down, final_norm, True)
    return h.reshape(bsz, t, d)
```

```python
import functools
import math

import jax
import jax.numpy as jnp
from jax import lax
from jax.experimental import pallas as pl
from jax.experimental.pallas import tpu as pltpu

F32 = jnp.float32
BF16 = jnp.bfloat16

D_MODEL = 1024
D_FF = 2816
NORM_EPS = 1e-6

GLA_HEADS = 4
GLA_DK = 512
GLA_DV = 1024
GLA_HEAD_K = 128
GLA_HEAD_V = 256
GLA_GATE_RANK = 16
GLA_TAU = 16.0
GLA_CHUNK = 64
GLA_MAIN = 2 * GLA_DK + 2 * GLA_DV

DIFF_HEADS = 8
DIFF_HEAD_DIM = 64
DIFF_V_DIM = 128

LOG2_E = math.log2(math.e)
LANES = 128
BF16_ROWS = 16
VMEM_LIMIT = 56 * 1024 * 1024

FFN_TM = 512
FFN_TF = 256
GLA_PIECE = 512
GLA_TM = 512
PROJ_TM = 512
ATT_TQ = 512
ATT_TK = 512
ATT_G = 2

NT_DIMS = (((1,), (1,)), ((), ()))
TN_DIMS = (((0,), (0,)), ((), ()))


def _rms(x, g):
    ms = jnp.mean(x * x, axis=-1, keepdims=True)
    return (x * lax.rsqrt(ms + NORM_EPS)) * g


def _silu(x):
    return x * (1.0 / (1.0 + jnp.exp(-x)))


def _resident(shape):
    nd = len(shape)
    return pl.BlockSpec(shape, lambda *_: (0,) * nd, pipeline_mode=pl.Buffered(1))


def _params(sem):
    return pltpu.CompilerParams(dimension_semantics=sem, vmem_limit_bytes=VMEM_LIMIT)


def _ffn_body(*refs, final_norm, fused_proj):
    if fused_proj:
        x_ref, o_ref, wout_ref, g_ref, win_ref, wdown_ref, gfin_ref, y_ref = refs
        x = x_ref[...] + jnp.dot(o_ref[...], wout_ref[...].astype(BF16), preferred_element_type=F32)
    else:
        x_ref, g_ref, win_ref, wdown_ref, gfin_ref, y_ref = refs
        x = x_ref[...]
    xn = _rms(x, g_ref[...]).astype(BF16)
    acc = None
    for c in range(D_FF // FFN_TF):
        lo = c * FFN_TF
        w_gate = win_ref[:, lo:lo + FFN_TF].astype(BF16)
        w_up = win_ref[:, D_FF + lo:D_FF + lo + FFN_TF].astype(BF16)
        gate = jnp.dot(xn, w_gate, preferred_element_type=F32)
        up = jnp.dot(xn, w_up, preferred_element_type=F32)
        h = (_silu(gate) * up).astype(BF16)
        part = jnp.dot(h, wdown_ref[lo:lo + FFN_TF, :].astype(BF16), preferred_element_type=F32)
        acc = part if acc is None else acc + part
    y = x + 0.5 * acc
    if final_norm:
        y = _rms(y, gfin_ref[...])
    y_ref[...] = y


def _ffn(x2d, g, w_in, w_down, g_final, final_norm, mixer_out=None, w_out=None):
    m = x2d.shape[0]
    row = pl.BlockSpec((FFN_TM, D_MODEL), lambda i: (i, 0))
    fused = mixer_out is not None
    args, specs = [x2d], [row]
    if fused:
        args += [mixer_out, w_out]
        specs += [row, _resident((D_MODEL, D_MODEL))]
    args += [g.reshape(1, -1), w_in, w_down, g_final.reshape(1, -1)]
    specs += [_resident((1, D_MODEL)), _resident((D_MODEL, 2 * D_FF)), _resident((D_FF, D_MODEL)),
              _resident((1, D_MODEL))]
    return pl.pallas_call(
        functools.partial(_ffn_body, final_norm=final_norm, fused_proj=fused),
        grid=(m // FFN_TM,),
        in_specs=specs,
        out_specs=row,
        out_shape=jax.ShapeDtypeStruct((m, D_MODEL), F32),
        compiler_params=_params(("arbitrary",)),
        name="ffn",
    )(*args)


def _gla_body(xn_ref, xp_ref, gmix_ref, win_ref, wg1_ref, wg2_ref, bg2_ref, gout_ref, wout_ref,
              o_ref, proj_ref, la_ref, state_ref, og_ref, *, tiles_per_seq):
    g = pl.program_id(0)
    nxt = g % 2
    prv = 1 - nxt

    @pl.when(g == 0)
    def _():
        proj_ref[...] = jnp.zeros_like(proj_ref)
        la_ref[...] = jnp.zeros_like(la_ref)

    @pl.when((g == 0) | ((g - 1) % tiles_per_seq == 0))
    def _():
        state_ref[...] = jnp.zeros_like(state_ref)

    hn = _rms(xn_ref[...], gmix_ref[...]).astype(BF16)

    rows = lax.broadcasted_iota(jnp.int32, (GLA_CHUNK, GLA_CHUNK), 0)
    cols = lax.broadcasted_iota(jnp.int32, (GLA_CHUNK, GLA_CHUNK), 1)
    causal = rows >= cols
    tril = causal.astype(BF16)
    scale = GLA_HEAD_K ** -0.5
    n_chunks = GLA_TM // GLA_CHUNK
    chunk_heads = [(n, h) for n in range(n_chunks) for h in range(GLA_HEADS)]

    def project(piece):
        if piece < GLA_MAIN // GLA_PIECE:
            c0 = piece * GLA_PIECE
            proj_ref[nxt, :, c0:c0 + GLA_PIECE] = jnp.dot(hn, win_ref[:, c0:c0 + GLA_PIECE],
                                                          preferred_element_type=F32)
        else:
            g_lr = jnp.dot(hn, wg1_ref[...], preferred_element_type=F32)
            z = jnp.dot(g_lr.astype(BF16), wg2_ref[...], preferred_element_type=F32) + bg2_ref[...]
            la_ref[nxt] = (jnp.minimum(z, 0.0) - jnp.log(1.0 + jnp.exp(-jnp.abs(z)))) * (1.0 / GLA_TAU)

    def rows_of(n):
        return slice(n * GLA_CHUNK, (n + 1) * GLA_CHUNK)

    def ks(h):
        return slice(h * GLA_HEAD_K, (h + 1) * GLA_HEAD_K)

    def vs(h):
        return slice(h * GLA_HEAD_V, (h + 1) * GLA_HEAD_V)

    project(0)
    cum = []
    for n in range(n_chunks):
        la = la_ref[prv, rows_of(n), :]
        la_hi = la.astype(BF16)
        la_r = la - la_hi.astype(F32)
        la_mid = la_r.astype(BF16)
        la_lo = (la_r - la_mid.astype(F32)).astype(BF16)
        cum.append(jnp.dot(tril, la_hi, preferred_element_type=F32)
                   + jnp.dot(tril, la_mid, preferred_element_type=F32)
                   + jnp.dot(tril, la_lo, preferred_element_type=F32))

    project(1)
    q_dec, k_inv, k_tail, decay, v = [], [], [], [], []
    for n in range(n_chunks):
        b = cum[n]
        b_last = b[GLA_CHUNK - 1:GLA_CHUNK]
        q = proj_ref[prv, rows_of(n), 0:GLA_DK]
        k = proj_ref[prv, rows_of(n), GLA_DK:2 * GLA_DK]
        q_dec.append(((q * scale) * jnp.exp(b)).astype(BF16))
        k_inv.append((k * jnp.exp(-b)).astype(BF16))
        k_tail.append((k * jnp.exp(b_last - b)).astype(BF16))
        decay.append(jnp.exp(b_last))
        v.append(proj_ref[prv, rows_of(n), 2 * GLA_DK:2 * GLA_DK + GLA_DV].astype(BF16))

    project(2)
    att = {}
    for n, h in chunk_heads:
        a = lax.dot_general(q_dec[n][:, ks(h)], k_inv[n][:, ks(h)], NT_DIMS, preferred_element_type=F32)
        att[n, h] = jnp.where(causal, a, 0.0).astype(BF16)

    project(3)
    upd = {}
    for n, h in chunk_heads:
        upd[n, h] = lax.dot_general(v[n][:, vs(h)], k_tail[n][:, ks(h)], TN_DIMS,
                                    preferred_element_type=F32)

    project(4)
    start = {}
    for h in range(GLA_HEADS):
        s_t = state_ref[h]
        for n in range(n_chunks):
            start[n, h] = s_t.astype(BF16)
            s_t = s_t * decay[n][:, ks(h)] + upd[n, h]
        state_ref[h] = s_t

    project(5)
    outs = {}
    for n, h in chunk_heads:
        o = jnp.dot(att[n, h], v[n][:, vs(h)], preferred_element_type=F32)
        outs[n, h] = o + lax.dot_general(q_dec[n][:, ks(h)], start[n, h], NT_DIMS, preferred_element_type=F32)

    project(6)
    for n, h in chunk_heads:
        r = proj_ref[prv, rows_of(n), 2 * GLA_DK + GLA_DV + h * GLA_HEAD_V:2 * GLA_DK + GLA_DV + (h + 1) * GLA_HEAD_V]
        o = _rms(outs[n, h], gout_ref[...])
        og_ref[rows_of(n), vs(h)] = (o * _silu(r)).astype(BF16)

    o_ref[...] = xp_ref[...] + jnp.dot(og_ref[...], wout_ref[...], preferred_element_type=F32)


def _gla_layer(x2d, g_mix, w_in, w_gate2, b_gate2, g_out, w_out, seq_len):
    m = x2d.shape[0]
    n_tiles = m // GLA_TM
    w_main = w_in[:, :GLA_MAIN].astype(BF16)
    w_g1 = jnp.pad(w_in[:, GLA_MAIN:], ((0, 0), (0, LANES - GLA_GATE_RANK))).astype(BF16)
    w_g2 = jnp.pad(w_gate2, ((0, LANES - GLA_GATE_RANK), (0, 0))).astype(BF16)
    nxt_spec = pl.BlockSpec((GLA_TM, D_MODEL), lambda g: (jnp.minimum(g, n_tiles - 1), 0))
    prv_spec = pl.BlockSpec((GLA_TM, D_MODEL), lambda g: (jnp.maximum(g - 1, 0), 0))
    return pl.pallas_call(
        functools.partial(_gla_body, tiles_per_seq=seq_len // GLA_TM),
        grid=(n_tiles + 1,),
        in_specs=[nxt_spec, prv_spec, _resident((1, D_MODEL)), _resident((D_MODEL, GLA_MAIN)),
                  _resident((D_MODEL, LANES)), _resident((LANES, GLA_DK)), _resident((1, GLA_DK)),
                  _resident((1, GLA_HEAD_V)), _resident((GLA_DV, D_MODEL))],
        out_specs=prv_spec,
        out_shape=jax.ShapeDtypeStruct((m, D_MODEL), F32),
        scratch_shapes=[pltpu.VMEM((2, GLA_TM, GLA_MAIN), F32),
                        pltpu.VMEM((2, GLA_TM, GLA_DK), F32),
                        pltpu.VMEM((GLA_HEADS, GLA_HEAD_V, GLA_HEAD_K), F32),
                        pltpu.VMEM((GLA_TM, GLA_DV), BF16)],
        compiler_params=_params(("arbitrary",)),
        name="gla_layer",
    )(x2d, x2d, g_mix.reshape(1, -1), w_main, w_g1, w_g2, b_gate2.reshape(1, -1),
      g_out.reshape(1, -1), w_out.astype(BF16))


def _qkv_body(x_ref, g_ref, wqt_ref, wk_ref, wvt_ref, qt_ref, k_ref, vt_ref):
    hn = _rms(x_ref[0], g_ref[...]).astype(BF16)
    qt = lax.dot_general(wqt_ref[...], hn, NT_DIMS, preferred_element_type=F32)
    qt_ref[0] = (qt * (DIFF_HEAD_DIM ** -0.5 * LOG2_E)).astype(BF16)
    k_ref[0] = jnp.dot(hn, wk_ref[...], preferred_element_type=F32).astype(BF16)
    vt = lax.dot_general(wvt_ref[...], hn, NT_DIMS, preferred_element_type=F32)
    for c in range(PROJ_TM // ATT_TK):
        vt_ref[0, c] = vt[:, c * ATT_TK:(c + 1) * ATT_TK].astype(BF16)


def _qkv_proj(x, g, w_in):
    bsz, t, _ = x.shape
    w_q, w_k, w_v = jnp.split(w_in.astype(BF16), 3, axis=1)
    return pl.pallas_call(
        _qkv_body,
        grid=(bsz, t // PROJ_TM),
        in_specs=[pl.BlockSpec((1, PROJ_TM, D_MODEL), lambda b, i: (b, i, 0)),
                  _resident((1, D_MODEL)), _resident((D_MODEL, D_MODEL)),
                  _resident((D_MODEL, D_MODEL)), _resident((D_MODEL, D_MODEL))],
        out_specs=[pl.BlockSpec((1, D_MODEL, PROJ_TM), lambda b, i: (b, 0, i)),
                   pl.BlockSpec((1, PROJ_TM, D_MODEL), lambda b, i: (b, i, 0)),
                   pl.BlockSpec((1, PROJ_TM // ATT_TK, D_MODEL, ATT_TK), lambda b, i: (b, i, 0, 0))],
        out_shape=[jax.ShapeDtypeStruct((bsz, D_MODEL, t), BF16),
                   jax.ShapeDtypeStruct((bsz, t, D_MODEL), BF16),
                   jax.ShapeDtypeStruct((bsz, t // ATT_TK, D_MODEL, ATT_TK), BF16)],
        compiler_params=_params(("arbitrary", "arbitrary")),
        name="diff_qkv",
    )(x, g.reshape(1, -1), w_q.T, w_k, w_v.T)


def _diff_attn_body(qt_ref, k_ref, vt_ref, lq1_ref, lk1_ref, lq2_ref, lk2_ref, gout_ref, o_ref, acc_ref,
                    *, lambda_init):
    i = pl.program_id(2)
    m2 = 2 * ATT_TQ
    feat = lax.broadcasted_iota(jnp.int32, (DIFF_V_DIM, ATT_TQ), 0)
    q2t = []
    for g in range(ATT_G):
        qt = qt_ref[0, g * DIFF_V_DIM:(g + 1) * DIFF_V_DIM, :].astype(F32)
        q2t.append(jnp.concatenate([jnp.where(feat < DIFF_HEAD_DIM, qt, 0.0),
                                    jnp.where(feat >= DIFF_HEAD_DIM, qt, 0.0)], axis=1).astype(BF16))

    heads = [slice(g * DIFF_V_DIM, (g + 1) * DIFF_V_DIM) for g in range(ATT_G)]

    def step(j, m_run, masked):
        scores = []
        for g in range(ATT_G):
            k = k_ref[0, pl.ds(j * ATT_TK, ATT_TK), heads[g]]
            s = jnp.dot(k, q2t[g], preferred_element_type=F32)
            if masked:
                kpos = lax.broadcasted_iota(jnp.int32, s.shape, 0)
                qpos = lax.broadcasted_iota(jnp.int32, s.shape, 1) % ATT_TQ
                s = jnp.where(qpos >= kpos, s, -jnp.inf)
            scores.append(s)
        probs = []
        for g in range(ATT_G):
            m_new = jnp.maximum(m_run[g], jnp.max(scores[g], axis=0, keepdims=True))
            probs.append((m_new, jnp.exp2(m_run[g] - m_new), jnp.exp2(scores[g] - m_new).astype(BF16)))
        for g in range(ATT_G):
            _, alpha, p = probs[g]
            vt1 = jnp.concatenate([vt_ref[0, j, heads[g], :], ones_rows], axis=0)
            acc_ref[g] = alpha * acc_ref[g] + jnp.dot(vt1, p, preferred_element_type=F32)
        return tuple(m_new for m_new, _, _ in probs)

    ones_rows = jnp.ones((BF16_ROWS, ATT_TK), BF16)
    acc_ref[...] = jnp.zeros_like(acc_ref)
    m_run = lax.fori_loop(0, i, functools.partial(step, masked=False),
                          tuple(jnp.full((1, m2), -jnp.inf, F32) for _ in range(ATT_G)))
    step(i, m_run, masked=True)

    lam = (jnp.exp(jnp.sum(lq1_ref[...] * lk1_ref[...])) - jnp.exp(jnp.sum(lq2_ref[...] * lk2_ref[...]))
           + lambda_init)
    for g in range(ATT_G):
        ot = acc_ref[g, :DIFF_V_DIM, :] / acc_ref[g, DIFF_V_DIM:DIFF_V_DIM + 1, :]
        o = (ot[:, :ATT_TQ] - lam * ot[:, ATT_TQ:]).T
        o = _rms(o, gout_ref[...]) * (1.0 - lambda_init)
        o_ref[0, :, g * DIFF_V_DIM:(g + 1) * DIFF_V_DIM] = o.astype(BF16)


def _diff_attn(qt, k, vt, lq1, lk1, lq2, lk2, g_out, lambda_init):
    bsz, t, _ = k.shape
    gw = ATT_G * DIFF_V_DIM
    lam_spec = _resident((1, DIFF_HEAD_DIM))
    return pl.pallas_call(
        functools.partial(_diff_attn_body, lambda_init=lambda_init),
        grid=(bsz, DIFF_HEADS // ATT_G, t // ATT_TQ),
        in_specs=[pl.BlockSpec((1, gw, ATT_TQ), lambda b, h, i: (b, h, i)),
                  pl.BlockSpec((1, t, gw), lambda b, h, i: (b, 0, h)),
                  pl.BlockSpec((1, t // ATT_TK, gw, ATT_TK), lambda b, h, i: (b, 0, h, 0)),
                  lam_spec, lam_spec, lam_spec, lam_spec, _resident((1, DIFF_V_DIM))],
        out_specs=pl.BlockSpec((1, ATT_TQ, gw), lambda b, h, i: (b, i, h)),
        out_shape=jax.ShapeDtypeStruct(k.shape, BF16),
        scratch_shapes=[pltpu.VMEM((ATT_G, DIFF_V_DIM + BF16_ROWS, 2 * ATT_TQ), F32)],
        compiler_params=_params(("arbitrary", "arbitrary", "arbitrary")),
        name="diff_attn",
    )(qt, k, vt, lq1.reshape(1, -1), lk1.reshape(1, -1), lq2.reshape(1, -1), lk2.reshape(1, -1),
      g_out.reshape(1, -1))


def kernel(x, l0_norm_ffn1, l0_ffn1_w_in, l0_ffn1_w_down, l0_norm_mix, l0_gla_w_in, l0_gla_w_gate2, l0_gla_b_gate2, l0_gla_norm_out, l0_gla_w_out, l0_norm_ffn2, l0_ffn2_w_in, l0_ffn2_w_down, l1_norm_ffn1, l1_ffn1_w_in, l1_ffn1_w_down, l1_norm_mix, l1_diff_w_in, l1_diff_lambda_q1, l1_diff_lambda_k1, l1_diff_lambda_q2, l1_diff_lambda_k2, l1_diff_norm_out, l1_diff_w_out, l1_norm_ffn2, l1_ffn2_w_in, l1_ffn2_w_down, final_norm):
    bsz, t, d = x.shape
    m = bsz * t
    lambda_init = 0.8 - 0.6 * math.exp(-0.3 * 1)

    h = _ffn(x.reshape(m, d), l0_norm_ffn1, l0_ffn1_w_in, l0_ffn1_w_down, final_norm, False)
    h = _gla_layer(h, l0_norm_mix, l0_gla_w_in, l0_gla_w_gate2, l0_gla_b_gate2,
                   l0_gla_norm_out, l0_gla_w_out, t)
    h = _ffn(h, l0_norm_ffn2, l0_ffn2_w_in, l0_ffn2_w_down, final_norm, False)

    h = _ffn(h, l1_norm_ffn1, l1_ffn1_w_in, l1_ffn1_w_down, final_norm, False)
    qt, k, vt = _qkv_proj(h.reshape(bsz, t, d), l1_norm_mix, l1_diff_w_in)
    o = _diff_attn(qt, k, vt, l1_diff_lambda_q1, l1_diff_lambda_k1, l1_diff_lambda_q2, l1_diff_lambda_k2,
                   l1_diff_norm_out, lambda_init)
    h = _ffn(h, l1_norm_ffn2, l1_ffn2_w_in, l1_ffn2_w_down, final_norm, True,
             mixer_out=o.reshape(m, d), w_out=l1_diff_w_out)
    return h.reshape(bsz, t, d)
```

```python
import functools
import math

import jax
import jax.numpy as jnp
from jax import lax
from jax.experimental import pallas as pl
from jax.experimental.pallas import tpu as pltpu

F32 = jnp.float32
BF16 = jnp.bfloat16

D_MODEL = 1024
D_FF = 2816
NORM_EPS = 1e-6

GLA_HEADS = 4
GLA_DK = 512
GLA_DV = 1024
GLA_HEAD_K = 128
GLA_HEAD_V = 256
GLA_GATE_RANK = 16
GLA_TAU = 16.0
GLA_CHUNK = 64
GLA_MAIN = 2 * GLA_DK + 2 * GLA_DV

DIFF_HEADS = 8
DIFF_HEAD_DIM = 64
DIFF_V_DIM = 128

LOG2_E = math.log2(math.e)
LANES = 128
BF16_ROWS = 16
VMEM_LIMIT = 56 * 1024 * 1024

FFN_TM = 512
FFN_TF = 256
GLA_PIECE = 512
GLA_TM = 512
PROJ_TM = 512
ATT_TQ = 512
ATT_TK = 512
ATT_G = 4
ATT_SPLIT = 1
ATT_SKEW = 1

NT_DIMS = (((1,), (1,)), ((), ()))
TN_DIMS = (((0,), (0,)), ((), ()))


def _rms(x, g):
    ms = jnp.mean(x * x, axis=-1, keepdims=True)
    return (x * lax.rsqrt(ms + NORM_EPS)) * g


def _silu(x):
    return x * (1.0 / (1.0 + jnp.exp(-x)))


def _resident(shape):
    nd = len(shape)
    return pl.BlockSpec(shape, lambda *_: (0,) * nd, pipeline_mode=pl.Buffered(1))


def _params(sem):
    return pltpu.CompilerParams(dimension_semantics=sem, vmem_limit_bytes=VMEM_LIMIT)


def _ffn_body(*refs, final_norm, fused_proj):
    if fused_proj:
        x_ref, o_ref, wout_ref, g_ref, win_ref, wdown_ref, gfin_ref, y_ref = refs
        x = x_ref[...] + jnp.dot(o_ref[...], wout_ref[...].astype(BF16), preferred_element_type=F32)
    else:
        x_ref, g_ref, win_ref, wdown_ref, gfin_ref, y_ref = refs
        x = x_ref[...]
    xn = _rms(x, g_ref[...]).astype(BF16)
    acc = None
    for c in range(D_FF // FFN_TF):
        lo = c * FFN_TF
        w_gate = win_ref[:, lo:lo + FFN_TF].astype(BF16)
        w_up = win_ref[:, D_FF + lo:D_FF + lo + FFN_TF].astype(BF16)
        gate = jnp.dot(xn, w_gate, preferred_element_type=F32)
        up = jnp.dot(xn, w_up, preferred_element_type=F32)
        h = (_silu(gate) * up).astype(BF16)
        part = jnp.dot(h, wdown_ref[lo:lo + FFN_TF, :].astype(BF16), preferred_element_type=F32)
        acc = part if acc is None else acc + part
    y = x + 0.5 * acc
    if final_norm:
        y = _rms(y, gfin_ref[...])
    y_ref[...] = y


def _ffn(x2d, g, w_in, w_down, g_final, final_norm, mixer_out=None, w_out=None):
    m = x2d.shape[0]
    row = pl.BlockSpec((FFN_TM, D_MODEL), lambda i: (i, 0))
    fused = mixer_out is not None
    args, specs = [x2d], [row]
    if fused:
        args += [mixer_out, w_out]
        specs += [row, _resident((D_MODEL, D_MODEL))]
    args += [g.reshape(1, -1), w_in, w_down, g_final.reshape(1, -1)]
    specs += [_resident((1, D_MODEL)), _resident((D_MODEL, 2 * D_FF)), _resident((D_FF, D_MODEL)),
              _resident((1, D_MODEL))]
    return pl.pallas_call(
        functools.partial(_ffn_body, final_norm=final_norm, fused_proj=fused),
        grid=(m // FFN_TM,),
        in_specs=specs,
        out_specs=row,
        out_shape=jax.ShapeDtypeStruct((m, D_MODEL), F32),
        compiler_params=_params(("arbitrary",)),
        name="ffn",
    )(*args)


def _gla_body(xn_ref, xp_ref, gmix_ref, win_ref, wg1_ref, wg2_ref, bg2_ref, gout_ref, wout_ref,
              o_ref, proj_ref, la_ref, state_ref, og_ref, *, tiles_per_seq):
    g = pl.program_id(0)
    nxt = g % 2
    prv = 1 - nxt

    @pl.when(g == 0)
    def _():
        proj_ref[...] = jnp.zeros_like(proj_ref)
        la_ref[...] = jnp.zeros_like(la_ref)

    @pl.when((g == 0) | ((g - 1) % tiles_per_seq == 0))
    def _():
        state_ref[...] = jnp.zeros_like(state_ref)

    hn = _rms(xn_ref[...], gmix_ref[...]).astype(BF16)

    rows = lax.broadcasted_iota(jnp.int32, (GLA_CHUNK, GLA_CHUNK), 0)
    cols = lax.broadcasted_iota(jnp.int32, (GLA_CHUNK, GLA_CHUNK), 1)
    causal = rows >= cols
    tril = causal.astype(BF16)
    scale = GLA_HEAD_K ** -0.5
    n_chunks = GLA_TM // GLA_CHUNK
    chunk_heads = [(n, h) for n in range(n_chunks) for h in range(GLA_HEADS)]

    def project(piece):
        if piece < GLA_MAIN // GLA_PIECE:
            c0 = piece * GLA_PIECE
            proj_ref[nxt, :, c0:c0 + GLA_PIECE] = jnp.dot(hn, win_ref[:, c0:c0 + GLA_PIECE],
                                                          preferred_element_type=F32)
        else:
            g_lr = jnp.dot(hn, wg1_ref[...], preferred_element_type=F32)
            z = jnp.dot(g_lr.astype(BF16), wg2_ref[...], preferred_element_type=F32) + bg2_ref[...]
            la_ref[nxt] = (jnp.minimum(z, 0.0) - jnp.log(1.0 + jnp.exp(-jnp.abs(z)))) * (1.0 / GLA_TAU)

    def rows_of(n):
        return slice(n * GLA_CHUNK, (n + 1) * GLA_CHUNK)

    def ks(h):
        return slice(h * GLA_HEAD_K, (h + 1) * GLA_HEAD_K)

    def vs(h):
        return slice(h * GLA_HEAD_V, (h + 1) * GLA_HEAD_V)

    project(0)
    cum = []
    for n in range(n_chunks):
        la = la_ref[prv, rows_of(n), :]
        la_hi = la.astype(BF16)
        la_r = la - la_hi.astype(F32)
        la_mid = la_r.astype(BF16)
        la_lo = (la_r - la_mid.astype(F32)).astype(BF16)
        cum.append(jnp.dot(tril, la_hi, preferred_element_type=F32)
                   + jnp.dot(tril, la_mid, preferred_element_type=F32)
                   + jnp.dot(tril, la_lo, preferred_element_type=F32))

    project(1)
    q_dec, k_inv, k_tail, decay, v = [], [], [], [], []
    for n in range(n_chunks):
        b = cum[n]
        b_last = b[GLA_CHUNK - 1:GLA_CHUNK]
        q = proj_ref[prv, rows_of(n), 0:GLA_DK]
        k = proj_ref[prv, rows_of(n), GLA_DK:2 * GLA_DK]
        q_dec.append(((q * scale) * jnp.exp(b)).astype(BF16))
        k_inv.append((k * jnp.exp(-b)).astype(BF16))
        k_tail.append((k * jnp.exp(b_last - b)).astype(BF16))
        decay.append(jnp.exp(b_last))
        v.append(proj_ref[prv, rows_of(n), 2 * GLA_DK:2 * GLA_DK + GLA_DV].astype(BF16))

    project(2)
    att = {}
    for n, h in chunk_heads:
        a = lax.dot_general(q_dec[n][:, ks(h)], k_inv[n][:, ks(h)], NT_DIMS, preferred_element_type=F32)
        att[n, h] = jnp.where(causal, a, 0.0).astype(BF16)

    project(3)
    upd = {}
    for n, h in chunk_heads:
        upd[n, h] = lax.dot_general(v[n][:, vs(h)], k_tail[n][:, ks(h)], TN_DIMS,
                                    preferred_element_type=F32)

    project(4)
    start = {}
    for h in range(GLA_HEADS):
        s_t = state_ref[h]
        for n in range(n_chunks):
            start[n, h] = s_t.astype(BF16)
            s_t = s_t * decay[n][:, ks(h)] + upd[n, h]
        state_ref[h] = s_t

    project(5)
    outs = {}
    for n, h in chunk_heads:
        o = jnp.dot(att[n, h], v[n][:, vs(h)], preferred_element_type=F32)
        outs[n, h] = o + lax.dot_general(q_dec[n][:, ks(h)], start[n, h], NT_DIMS, preferred_element_type=F32)

    project(6)
    for n, h in chunk_heads:
        r = proj_ref[prv, rows_of(n), 2 * GLA_DK + GLA_DV + h * GLA_HEAD_V:2 * GLA_DK + GLA_DV + (h + 1) * GLA_HEAD_V]
        o = _rms(outs[n, h], gout_ref[...])
        og_ref[rows_of(n), vs(h)] = (o * _silu(r)).astype(BF16)

    o_ref[...] = xp_ref[...] + jnp.dot(og_ref[...], wout_ref[...], preferred_element_type=F32)


def _gla_layer(x2d, g_mix, w_in, w_gate2, b_gate2, g_out, w_out, seq_len):
    m = x2d.shape[0]
    n_tiles = m // GLA_TM
    w_main = w_in[:, :GLA_MAIN].astype(BF16)
    w_g1 = jnp.pad(w_in[:, GLA_MAIN:], ((0, 0), (0, LANES - GLA_GATE_RANK))).astype(BF16)
    w_g2 = jnp.pad(w_gate2, ((0, LANES - GLA_GATE_RANK), (0, 0))).astype(BF16)
    nxt_spec = pl.BlockSpec((GLA_TM, D_MODEL), lambda g: (jnp.minimum(g, n_tiles - 1), 0))
    prv_spec = pl.BlockSpec((GLA_TM, D_MODEL), lambda g: (jnp.maximum(g - 1, 0), 0))
    return pl.pallas_call(
        functools.partial(_gla_body, tiles_per_seq=seq_len // GLA_TM),
        grid=(n_tiles + 1,),
        in_specs=[nxt_spec, prv_spec, _resident((1, D_MODEL)), _resident((D_MODEL, GLA_MAIN)),
                  _resident((D_MODEL, LANES)), _resident((LANES, GLA_DK)), _resident((1, GLA_DK)),
                  _resident((1, GLA_HEAD_V)), _resident((GLA_DV, D_MODEL))],
        out_specs=prv_spec,
        out_shape=jax.ShapeDtypeStruct((m, D_MODEL), F32),
        scratch_shapes=[pltpu.VMEM((2, GLA_TM, GLA_MAIN), F32),
                        pltpu.VMEM((2, GLA_TM, GLA_DK), F32),
                        pltpu.VMEM((GLA_HEADS, GLA_HEAD_V, GLA_HEAD_K), F32),
                        pltpu.VMEM((GLA_TM, GLA_DV), BF16)],
        compiler_params=_params(("arbitrary",)),
        name="gla_layer",
    )(x2d, x2d, g_mix.reshape(1, -1), w_main, w_g1, w_g2, b_gate2.reshape(1, -1),
      g_out.reshape(1, -1), w_out.astype(BF16))


def _qkv_body(x_ref, g_ref, wqt_ref, wk_ref, wvt_ref, qt_ref, k_ref, vt_ref):
    hn = _rms(x_ref[0], g_ref[...]).astype(BF16)
    qt = lax.dot_general(wqt_ref[...], hn, NT_DIMS, preferred_element_type=F32)
    qt_ref[0] = (qt * (DIFF_HEAD_DIM ** -0.5 * LOG2_E)).astype(BF16)
    k_ref[0] = jnp.dot(hn, wk_ref[...], preferred_element_type=F32).astype(BF16)
    vt = lax.dot_general(wvt_ref[...], hn, NT_DIMS, preferred_element_type=F32)
    for c in range(PROJ_TM // ATT_TK):
        vt_ref[0, c] = vt[:, c * ATT_TK:(c + 1) * ATT_TK].astype(BF16)


def _qkv_proj(x, g, w_in):
    bsz, t, _ = x.shape
    w_q, w_k, w_v = jnp.split(w_in.astype(BF16), 3, axis=1)
    return pl.pallas_call(
        _qkv_body,
        grid=(bsz, t // PROJ_TM),
        in_specs=[pl.BlockSpec((1, PROJ_TM, D_MODEL), lambda b, i: (b, i, 0)),
                  _resident((1, D_MODEL)), _resident((D_MODEL, D_MODEL)),
                  _resident((D_MODEL, D_MODEL)), _resident((D_MODEL, D_MODEL))],
        out_specs=[pl.BlockSpec((1, D_MODEL, PROJ_TM), lambda b, i: (b, 0, i)),
                   pl.BlockSpec((1, PROJ_TM, D_MODEL), lambda b, i: (b, i, 0)),
                   pl.BlockSpec((1, PROJ_TM // ATT_TK, D_MODEL, ATT_TK), lambda b, i: (b, i, 0, 0))],
        out_shape=[jax.ShapeDtypeStruct((bsz, D_MODEL, t), BF16),
                   jax.ShapeDtypeStruct((bsz, t, D_MODEL), BF16),
                   jax.ShapeDtypeStruct((bsz, t // ATT_TK, D_MODEL, ATT_TK), BF16)],
        compiler_params=_params(("arbitrary", "arbitrary")),
        name="diff_qkv",
    )(x, g.reshape(1, -1), w_q.T, w_k, w_v.T)


def _diff_attn_body(qt_ref, k_ref, vt_ref, lq1_ref, lk1_ref, lq2_ref, lk2_ref, gout_ref, o_ref, acc_ref,
                    *, lambda_init):
    i = pl.program_id(2)
    m2 = 2 * ATT_TQ
    feat = lax.broadcasted_iota(jnp.int32, (DIFF_V_DIM, ATT_TQ), 0)
    q2t = []
    for g in range(ATT_G):
        qt = qt_ref[0, g * DIFF_V_DIM:(g + 1) * DIFF_V_DIM, :].astype(F32)
        q2t.append(jnp.concatenate([jnp.where(feat < DIFF_HEAD_DIM, qt, 0.0),
                                    jnp.where(feat >= DIFF_HEAD_DIM, qt, 0.0)], axis=1).astype(BF16))

    heads = [slice(g * DIFF_V_DIM, (g + 1) * DIFF_V_DIM) for g in range(ATT_G)]

    unit_w = m2 // ATT_SPLIT
    units = [(g, slice(c * unit_w, (c + 1) * unit_w)) for g in range(ATT_G) for c in range(ATT_SPLIT)]

    def step(j, m_run, masked):
        scores, probs = {}, {}

        def score(u):
            g, lanes = units[u]
            k = k_ref[0, pl.ds(j * ATT_TK, ATT_TK), heads[g]]
            s = jnp.dot(k, q2t[g][:, lanes], preferred_element_type=F32)
            if masked:
                kpos = lax.broadcasted_iota(jnp.int32, s.shape, 0)
                qpos = (lax.broadcasted_iota(jnp.int32, s.shape, 1) + lanes.start) % ATT_TQ
                s = jnp.where(qpos >= kpos, s, -jnp.inf)
            scores[u] = s

        def softmax(u):
            m_new = jnp.maximum(m_run[u], jnp.max(scores[u], axis=0, keepdims=True))
            probs[u] = (m_new, jnp.exp2(m_run[u] - m_new), jnp.exp2(scores[u] - m_new).astype(BF16))

        def value(u):
            g, lanes = units[u]
            _, alpha, p = probs[u]
            vt1 = jnp.concatenate([vt_ref[0, j, heads[g], :], ones_rows], axis=0)
            acc_ref[g, :, lanes] = (alpha * acc_ref[g, :, lanes]
                                    + jnp.dot(vt1, p, preferred_element_type=F32))

        for t in range(len(units) + 2 * ATT_SKEW):
            if t < len(units):
                score(t)
            if 0 <= t - ATT_SKEW < len(units):
                softmax(t - ATT_SKEW)
            if 0 <= t - 2 * ATT_SKEW < len(units):
                value(t - 2 * ATT_SKEW)
        return tuple(probs[u][0] for u in range(len(units)))

    ones_rows = jnp.ones((BF16_ROWS, ATT_TK), BF16)
    acc_ref[...] = jnp.zeros_like(acc_ref)
    m_run = lax.fori_loop(0, i, functools.partial(step, masked=False),
                          tuple(jnp.full((1, unit_w), -jnp.inf, F32) for _ in units))
    step(i, m_run, masked=True)

    lam = (jnp.exp(jnp.sum(lq1_ref[...] * lk1_ref[...])) - jnp.exp(jnp.sum(lq2_ref[...] * lk2_ref[...]))
           + lambda_init)
    for g in range(ATT_G):
        ot = acc_ref[g, :DIFF_V_DIM, :] / acc_ref[g, DIFF_V_DIM:DIFF_V_DIM + 1, :]
        o = (ot[:, :ATT_TQ] - lam * ot[:, ATT_TQ:]).T
        o = _rms(o, gout_ref[...]) * (1.0 - lambda_init)
        o_ref[0, :, g * DIFF_V_DIM:(g + 1) * DIFF_V_DIM] = o.astype(BF16)


def _diff_attn(qt, k, vt, lq1, lk1, lq2, lk2, g_out, lambda_init):
    bsz, t, _ = k.shape
    gw = ATT_G * DIFF_V_DIM
    lam_spec = _resident((1, DIFF_HEAD_DIM))
    return pl.pallas_call(
        functools.partial(_diff_attn_body, lambda_init=lambda_init),
        grid=(bsz, DIFF_HEADS // ATT_G, t // ATT_TQ),
        in_specs=[pl.BlockSpec((1, gw, ATT_TQ), lambda b, h, i: (b, h, i)),
                  pl.BlockSpec((1, t, gw), lambda b, h, i: (b, 0, h)),
                  pl.BlockSpec((1, t // ATT_TK, gw, ATT_TK), lambda b, h, i: (b, 0, h, 0)),
                  lam_spec, lam_spec, lam_spec, lam_spec, _resident((1, DIFF_V_DIM))],
        out_specs=pl.BlockSpec((1, ATT_TQ, gw), lambda b, h, i: (b, i, h)),
        out_shape=jax.ShapeDtypeStruct(k.shape, BF16),
        scratch_shapes=[pltpu.VMEM((ATT_G, DIFF_V_DIM + BF16_ROWS, 2 * ATT_TQ), F32)],
        compiler_params=_params(("arbitrary", "arbitrary", "arbitrary")),
        name="diff_attn",
    )(qt, k, vt, lq1.reshape(1, -1), lk1.reshape(1, -1), lq2.reshape(1, -1), lk2.reshape(1, -1),
      g_out.reshape(1, -1))


def kernel(x, l0_norm_ffn1, l0_ffn1_w_in, l0_ffn1_w_down, l0_norm_mix, l0_gla_w_in, l0_gla_w_gate2, l0_gla_b_gate2, l0_gla_norm_out, l0_gla_w_out, l0_norm_ffn2, l0_ffn2_w_in, l0_ffn2_w_down, l1_norm_ffn1, l1_ffn1_w_in, l1_ffn1_w_down, l1_norm_mix, l1_diff_w_in, l1_diff_lambda_q1, l1_diff_lambda_k1, l1_diff_lambda_q2, l1_diff_lambda_k2, l1_diff_norm_out, l1_diff_w_out, l1_norm_ffn2, l1_ffn2_w_in, l1_ffn2_w_down, final_norm):
    bsz, t, d = x.shape
    m = bsz * t
    lambda_init = 0.8 - 0.6 * math.exp(-0.3 * 1)

    h = _ffn(x.reshape(m, d), l0_norm_ffn1, l0_ffn1_w_in, l0_ffn1_w_down, final_norm, False)
    h = _gla_layer(h, l0_norm_mix, l0_gla_w_in, l0_gla_w_gate2, l0_gla_b_gate2,
                   l0_gla_norm_out, l0_gla_w_out, t)
    h = _ffn(h, l0_norm_ffn2, l0_ffn2_w_in, l0_ffn2_w_down, final_norm, False)

    h = _ffn(h, l1_norm_ffn1, l1_ffn1_w_in, l1_ffn1_w_down, final_norm, False)
    qt, k, vt = _qkv_proj(h.reshape(bsz, t, d), l1_norm_mix, l1_diff_w_in)
    o = _diff_attn(qt, k, vt, l1_diff_lambda_q1, l1_diff_lambda_k1, l1_diff_lambda_q2, l1_diff_lambda_k2,
                   l1_diff_norm_out, lambda_init)
    h = _ffn(h, l1_norm_ffn2, l1_ffn2_w_in, l1_ffn2_w_down, final_norm, True,
             mixer_out=o.reshape(m, d), w_out=l1_diff_w_out)
    return h.reshape(bsz, t, d)
```

```python
import functools
import math

import jax
import jax.numpy as jnp
from jax import lax
from jax.experimental import pallas as pl
from jax.experimental.pallas import tpu as pltpu

F32 = jnp.float32
BF16 = jnp.bfloat16

D_MODEL = 1024
D_FF = 2816
NORM_EPS = 1e-6

GLA_HEADS = 4
GLA_DK = 512
GLA_DV = 1024
GLA_HEAD_K = 128
GLA_HEAD_V = 256
GLA_GATE_RANK = 16
GLA_TAU = 16.0
GLA_CHUNK = 64
GLA_MAIN = 2 * GLA_DK + 2 * GLA_DV

DIFF_HEADS = 8
DIFF_HEAD_DIM = 64
DIFF_V_DIM = 128

LOG2_E = math.log2(math.e)
LANES = 128
BF16_ROWS = 16
VMEM_LIMIT = 56 * 1024 * 1024

FFN_TM = 512
FFN_TF = 256
GLA_PIECE = 512
GLA_TM = 512
PROJ_TM = 512
ATT_TQ = 512
ATT_TK = 512
ATT_G = 4
ATT_SPLIT = 1
ATT_SKEW_SM = 1
ATT_SKEW_PV = 1

NT_DIMS = (((1,), (1,)), ((), ()))
TN_DIMS = (((0,), (0,)), ((), ()))


def _rms(x, g):
    ms = jnp.mean(x * x, axis=-1, keepdims=True)
    return (x * lax.rsqrt(ms + NORM_EPS)) * g


def _silu(x):
    return x * (1.0 / (1.0 + jnp.exp(-x)))


def _resident(shape):
    nd = len(shape)
    return pl.BlockSpec(shape, lambda *_: (0,) * nd, pipeline_mode=pl.Buffered(1))


def _params(sem):
    return pltpu.CompilerParams(dimension_semantics=sem, vmem_limit_bytes=VMEM_LIMIT)


def _ffn_body(*refs, final_norm, fused_proj):
    if fused_proj:
        x_ref, o_ref, wout_ref, g_ref, win_ref, wdown_ref, gfin_ref, y_ref = refs
        x = x_ref[...] + jnp.dot(o_ref[...], wout_ref[...].astype(BF16), preferred_element_type=F32)
    else:
        x_ref, g_ref, win_ref, wdown_ref, gfin_ref, y_ref = refs
        x = x_ref[...]
    xn = _rms(x, g_ref[...]).astype(BF16)
    acc = None
    for c in range(D_FF // FFN_TF):
        lo = c * FFN_TF
        w_gate = win_ref[:, lo:lo + FFN_TF].astype(BF16)
        w_up = win_ref[:, D_FF + lo:D_FF + lo + FFN_TF].astype(BF16)
        gate = jnp.dot(xn, w_gate, preferred_element_type=F32)
        up = jnp.dot(xn, w_up, preferred_element_type=F32)
        h = (_silu(gate) * up).astype(BF16)
        part = jnp.dot(h, wdown_ref[lo:lo + FFN_TF, :].astype(BF16), preferred_element_type=F32)
        acc = part if acc is None else acc + part
    y = x + 0.5 * acc
    if final_norm:
        y = _rms(y, gfin_ref[...])
    y_ref[...] = y


def _ffn(x2d, g, w_in, w_down, g_final, final_norm, mixer_out=None, w_out=None):
    m = x2d.shape[0]
    row = pl.BlockSpec((FFN_TM, D_MODEL), lambda i: (i, 0))
    fused = mixer_out is not None
    args, specs = [x2d], [row]
    if fused:
        args += [mixer_out, w_out]
        specs += [row, _resident((D_MODEL, D_MODEL))]
    args += [g.reshape(1, -1), w_in, w_down, g_final.reshape(1, -1)]
    specs += [_resident((1, D_MODEL)), _resident((D_MODEL, 2 * D_FF)), _resident((D_FF, D_MODEL)),
              _resident((1, D_MODEL))]
    return pl.pallas_call(
        functools.partial(_ffn_body, final_norm=final_norm, fused_proj=fused),
        grid=(m // FFN_TM,),
        in_specs=specs,
        out_specs=row,
        out_shape=jax.ShapeDtypeStruct((m, D_MODEL), F32),
        compiler_params=_params(("arbitrary",)),
        name="ffn",
    )(*args)


def _gla_body(xn_ref, xp_ref, gmix_ref, win_ref, wg1_ref, wg2_ref, bg2_ref, gout_ref, wout_ref,
              o_ref, proj_ref, la_ref, state_ref, og_ref, *, tiles_per_seq):
    g = pl.program_id(0)
    nxt = g % 2
    prv = 1 - nxt

    @pl.when(g == 0)
    def _():
        proj_ref[...] = jnp.zeros_like(proj_ref)
        la_ref[...] = jnp.zeros_like(la_ref)

    @pl.when((g == 0) | ((g - 1) % tiles_per_seq == 0))
    def _():
        state_ref[...] = jnp.zeros_like(state_ref)

    hn = _rms(xn_ref[...], gmix_ref[...]).astype(BF16)

    rows = lax.broadcasted_iota(jnp.int32, (GLA_CHUNK, GLA_CHUNK), 0)
    cols = lax.broadcasted_iota(jnp.int32, (GLA_CHUNK, GLA_CHUNK), 1)
    causal = rows >= cols
    tril = causal.astype(BF16)
    scale = GLA_HEAD_K ** -0.5
    n_chunks = GLA_TM // GLA_CHUNK

    def project(piece):
        if piece < GLA_MAIN // GLA_PIECE:
            c0 = piece * GLA_PIECE
            proj_ref[nxt, :, c0:c0 + GLA_PIECE] = jnp.dot(hn, win_ref[:, c0:c0 + GLA_PIECE],
                                                          preferred_element_type=F32)
        else:
            g_lr = jnp.dot(hn, wg1_ref[...], preferred_element_type=F32)
            z = jnp.dot(g_lr.astype(BF16), wg2_ref[...], preferred_element_type=F32) + bg2_ref[...]
            la_ref[nxt] = (jnp.minimum(z, 0.0) - jnp.log(1.0 + jnp.exp(-jnp.abs(z)))) * (1.0 / GLA_TAU)

    def rows_of(n):
        return slice(n * GLA_CHUNK, (n + 1) * GLA_CHUNK)

    def ks(h):
        return slice(h * GLA_HEAD_K, (h + 1) * GLA_HEAD_K)

    def vs(h):
        return slice(h * GLA_HEAD_V, (h + 1) * GLA_HEAD_V)

    cum, q_dec, k_inv, k_tail, decay, v, att, upd, start, outs = ({} for _ in range(10))
    carry = [state_ref[h] for h in range(GLA_HEADS)]

    def cumsum(n):
        la = la_ref[prv, rows_of(n), :]
        la_hi = la.astype(BF16)
        la_r = la - la_hi.astype(F32)
        la_mid = la_r.astype(BF16)
        la_lo = (la_r - la_mid.astype(F32)).astype(BF16)
        cum[n] = (jnp.dot(tril, la_hi, preferred_element_type=F32)
                  + jnp.dot(tril, la_mid, preferred_element_type=F32)
                  + jnp.dot(tril, la_lo, preferred_element_type=F32))

    def decays(n):
        b = cum[n]
        b_last = b[GLA_CHUNK - 1:GLA_CHUNK]
        q = proj_ref[prv, rows_of(n), 0:GLA_DK]
        k = proj_ref[prv, rows_of(n), GLA_DK:2 * GLA_DK]
        q_dec[n] = ((q * scale) * jnp.exp(b)).astype(BF16)
        k_inv[n] = (k * jnp.exp(-b)).astype(BF16)
        k_tail[n] = (k * jnp.exp(b_last - b)).astype(BF16)
        decay[n] = jnp.exp(b_last)
        v[n] = proj_ref[prv, rows_of(n), 2 * GLA_DK:2 * GLA_DK + GLA_DV].astype(BF16)

    def intra(n):
        for h in range(GLA_HEADS):
            a = lax.dot_general(q_dec[n][:, ks(h)], k_inv[n][:, ks(h)], NT_DIMS, preferred_element_type=F32)
            att[n, h] = jnp.where(causal, a, 0.0).astype(BF16)
            upd[n, h] = lax.dot_general(v[n][:, vs(h)], k_tail[n][:, ks(h)], TN_DIMS,
                                        preferred_element_type=F32)

    def scan(n):
        for h in range(GLA_HEADS):
            start[n, h] = carry[h].astype(BF16)
            carry[h] = carry[h] * decay[n][:, ks(h)] + upd[n, h]

    def outputs(n):
        for h in range(GLA_HEADS):
            o = jnp.dot(att[n, h], v[n][:, vs(h)], preferred_element_type=F32)
            outs[n, h] = o + lax.dot_general(q_dec[n][:, ks(h)], start[n, h], NT_DIMS,
                                             preferred_element_type=F32)

    def gate(n):
        for h in range(GLA_HEADS):
            r0 = 2 * GLA_DK + GLA_DV + h * GLA_HEAD_V
            r = proj_ref[prv, rows_of(n), r0:r0 + GLA_HEAD_V]
            o = _rms(outs[n, h], gout_ref[...])
            og_ref[rows_of(n), vs(h)] = (o * _silu(r)).astype(BF16)

    phases = [cumsum, decays, intra, scan, outputs, gate]
    for t in range(n_chunks + len(phases) - 1):
        if t <= GLA_MAIN // GLA_PIECE:
            project(t)
        for p, phase in enumerate(phases):
            if 0 <= t - p < n_chunks:
                phase(t - p)
    for h in range(GLA_HEADS):
        state_ref[h] = carry[h]

    o_ref[...] = xp_ref[...] + jnp.dot(og_ref[...], wout_ref[...], preferred_element_type=F32)


def _gla_layer(x2d, g_mix, w_in, w_gate2, b_gate2, g_out, w_out, seq_len):
    m = x2d.shape[0]
    n_tiles = m // GLA_TM
    w_main = w_in[:, :GLA_MAIN].astype(BF16)
    w_g1 = jnp.pad(w_in[:, GLA_MAIN:], ((0, 0), (0, LANES - GLA_GATE_RANK))).astype(BF16)
    w_g2 = jnp.pad(w_gate2, ((0, LANES - GLA_GATE_RANK), (0, 0))).astype(BF16)
    nxt_spec = pl.BlockSpec((GLA_TM, D_MODEL), lambda g: (jnp.minimum(g, n_tiles - 1), 0))
    prv_spec = pl.BlockSpec((GLA_TM, D_MODEL), lambda g: (jnp.maximum(g - 1, 0), 0))
    return pl.pallas_call(
        functools.partial(_gla_body, tiles_per_seq=seq_len // GLA_TM),
        grid=(n_tiles + 1,),
        in_specs=[nxt_spec, prv_spec, _resident((1, D_MODEL)), _resident((D_MODEL, GLA_MAIN)),
                  _resident((D_MODEL, LANES)), _resident((LANES, GLA_DK)), _resident((1, GLA_DK)),
                  _resident((1, GLA_HEAD_V)), _resident((GLA_DV, D_MODEL))],
        out_specs=prv_spec,
        out_shape=jax.ShapeDtypeStruct((m, D_MODEL), F32),
        scratch_shapes=[pltpu.VMEM((2, GLA_TM, GLA_MAIN), F32),
                        pltpu.VMEM((2, GLA_TM, GLA_DK), F32),
                        pltpu.VMEM((GLA_HEADS, GLA_HEAD_V, GLA_HEAD_K), F32),
                        pltpu.VMEM((GLA_TM, GLA_DV), BF16)],
        compiler_params=_params(("arbitrary",)),
        name="gla_layer",
    )(x2d, x2d, g_mix.reshape(1, -1), w_main, w_g1, w_g2, b_gate2.reshape(1, -1),
      g_out.reshape(1, -1), w_out.astype(BF16))


def _qkv_body(x_ref, g_ref, wqt_ref, wk_ref, wvt_ref, qt_ref, k_ref, vt_ref):
    hn = _rms(x_ref[0], g_ref[...]).astype(BF16)
    qt = lax.dot_general(wqt_ref[...], hn, NT_DIMS, preferred_element_type=F32)
    qt_ref[0] = (qt * (DIFF_HEAD_DIM ** -0.5 * LOG2_E)).astype(BF16)
    k_ref[0] = jnp.dot(hn, wk_ref[...], preferred_element_type=F32).astype(BF16)
    vt = lax.dot_general(wvt_ref[...], hn, NT_DIMS, preferred_element_type=F32)
    for c in range(PROJ_TM // ATT_TK):
        vt_ref[0, c] = vt[:, c * ATT_TK:(c + 1) * ATT_TK].astype(BF16)


def _qkv_proj(x, g, w_in):
    bsz, t, _ = x.shape
    w_q, w_k, w_v = jnp.split(w_in.astype(BF16), 3, axis=1)
    return pl.pallas_call(
        _qkv_body,
        grid=(bsz, t // PROJ_TM),
        in_specs=[pl.BlockSpec((1, PROJ_TM, D_MODEL), lambda b, i: (b, i, 0)),
                  _resident((1, D_MODEL)), _resident((D_MODEL, D_MODEL)),
                  _resident((D_MODEL, D_MODEL)), _resident((D_MODEL, D_MODEL))],
        out_specs=[pl.BlockSpec((1, D_MODEL, PROJ_TM), lambda b, i: (b, 0, i)),
                   pl.BlockSpec((1, PROJ_TM, D_MODEL), lambda b, i: (b, i, 0)),
                   pl.BlockSpec((1, PROJ_TM // ATT_TK, D_MODEL, ATT_TK), lambda b, i: (b, i, 0, 0))],
        out_shape=[jax.ShapeDtypeStruct((bsz, D_MODEL, t), BF16),
                   jax.ShapeDtypeStruct((bsz, t, D_MODEL), BF16),
                   jax.ShapeDtypeStruct((bsz, t // ATT_TK, D_MODEL, ATT_TK), BF16)],
        compiler_params=_params(("arbitrary", "arbitrary")),
        name="diff_qkv",
    )(x, g.reshape(1, -1), w_q.T, w_k, w_v.T)


def _diff_attn_body(qt_ref, k_ref, vt_ref, lq1_ref, lk1_ref, lq2_ref, lk2_ref, gout_ref, o_ref, acc_ref,
                    *, lambda_init):
    i = pl.program_id(2)
    m2 = 2 * ATT_TQ
    feat = lax.broadcasted_iota(jnp.int32, (DIFF_V_DIM, ATT_TQ), 0)
    q2t = []
    for g in range(ATT_G):
        qt = qt_ref[0, g * DIFF_V_DIM:(g + 1) * DIFF_V_DIM, :].astype(F32)
        q2t.append(jnp.concatenate([jnp.where(feat < DIFF_HEAD_DIM, qt, 0.0),
                                    jnp.where(feat >= DIFF_HEAD_DIM, qt, 0.0)], axis=1).astype(BF16))

    heads = [slice(g * DIFF_V_DIM, (g + 1) * DIFF_V_DIM) for g in range(ATT_G)]

    unit_w = m2 // ATT_SPLIT
    units = [(g, slice(c * unit_w, (c + 1) * unit_w)) for g in range(ATT_G) for c in range(ATT_SPLIT)]

    def step(j, m_run, diag_offset=None):
        scores, probs = {}, {}

        def score(u):
            g, lanes = units[u]
            k = k_ref[0, pl.ds(j * ATT_TK, ATT_TK), heads[g]]
            s = jnp.dot(k, q2t[g][:, lanes], preferred_element_type=F32)
            if diag_offset is not None:
                kpos = lax.broadcasted_iota(jnp.int32, s.shape, 0) + diag_offset
                qpos = (lax.broadcasted_iota(jnp.int32, s.shape, 1) + lanes.start) % ATT_TQ
                s = jnp.where(qpos >= kpos, s, -jnp.inf)
            scores[u] = s

        def softmax(u):
            m_new = jnp.maximum(m_run[u], jnp.max(scores[u], axis=0, keepdims=True))
            probs[u] = (m_new, jnp.exp2(m_run[u] - m_new), jnp.exp2(scores[u] - m_new).astype(BF16))

        def value(u):
            g, lanes = units[u]
            _, alpha, p = probs[u]
            vt1 = jnp.concatenate([vt_ref[0, j, heads[g], :], ones_rows], axis=0)
            acc_ref[g, :, lanes] = (alpha * acc_ref[g, :, lanes]
                                    + jnp.dot(vt1, p, preferred_element_type=F32))

        for t in range(len(units) + ATT_SKEW_PV):
            if t < len(units):
                score(t)
            if 0 <= t - ATT_SKEW_SM < len(units):
                softmax(t - ATT_SKEW_SM)
            if 0 <= t - ATT_SKEW_PV < len(units):
                value(t - ATT_SKEW_PV)
        return tuple(probs[u][0] for u in range(len(units)))

    ones_rows = jnp.ones((BF16_ROWS, ATT_TK), BF16)
    acc_ref[...] = jnp.zeros_like(acc_ref)
    blocks_per_tile = ATT_TQ // ATT_TK
    m_run = lax.fori_loop(0, i * blocks_per_tile, step,
                          tuple(jnp.full((1, unit_w), -jnp.inf, F32) for _ in units))
    for d in range(blocks_per_tile):
        m_run = step(i * blocks_per_tile + d, m_run, diag_offset=d * ATT_TK)

    lam = (jnp.exp(jnp.sum(lq1_ref[...] * lk1_ref[...])) - jnp.exp(jnp.sum(lq2_ref[...] * lk2_ref[...]))
           + lambda_init)
    for g in range(ATT_G):
        ot = acc_ref[g, :DIFF_V_DIM, :] / acc_ref[g, DIFF_V_DIM:DIFF_V_DIM + 1, :]
        o = (ot[:, :ATT_TQ] - lam * ot[:, ATT_TQ:]).T
        o = _rms(o, gout_ref[...]) * (1.0 - lambda_init)
        o_ref[0, :, g * DIFF_V_DIM:(g + 1) * DIFF_V_DIM] = o.astype(BF16)


def _diff_attn(qt, k, vt, lq1, lk1, lq2, lk2, g_out, lambda_init):
    bsz, t, _ = k.shape
    gw = ATT_G * DIFF_V_DIM
    lam_spec = _resident((1, DIFF_HEAD_DIM))
    return pl.pallas_call(
        functools.partial(_diff_attn_body, lambda_init=lambda_init),
        grid=(bsz, DIFF_HEADS // ATT_G, t // ATT_TQ),
        in_specs=[pl.BlockSpec((1, gw, ATT_TQ), lambda b, h, i: (b, h, i)),
                  pl.BlockSpec((1, t, gw), lambda b, h, i: (b, 0, h)),
                  pl.BlockSpec((1, t // ATT_TK, gw, ATT_TK), lambda b, h, i: (b, 0, h, 0)),
                  lam_spec, lam_spec, lam_spec, lam_spec, _resident((1, DIFF_V_DIM))],
        out_specs=pl.BlockSpec((1, ATT_TQ, gw), lambda b, h, i: (b, i, h)),
        out_shape=jax.ShapeDtypeStruct(k.shape, BF16),
        scratch_shapes=[pltpu.VMEM((ATT_G, DIFF_V_DIM + BF16_ROWS, 2 * ATT_TQ), F32)],
        compiler_params=_params(("arbitrary", "arbitrary", "arbitrary")),
        name="diff_attn",
    )(qt, k, vt, lq1.reshape(1, -1), lk1.reshape(1, -1), lq2.reshape(1, -1), lk2.reshape(1, -1),
      g_out.reshape(1, -1))


def kernel(x, l0_norm_ffn1, l0_ffn1_w_in, l0_ffn1_w_down, l0_norm_mix, l0_gla_w_in, l0_gla_w_gate2, l0_gla_b_gate2, l0_gla_norm_out, l0_gla_w_out, l0_norm_ffn2, l0_ffn2_w_in, l0_ffn2_w_down, l1_norm_ffn1, l1_ffn1_w_in, l1_ffn1_w_down, l1_norm_mix, l1_diff_w_in, l1_diff_lambda_q1, l1_diff_lambda_k1, l1_diff_lambda_q2, l1_diff_lambda_k2, l1_diff_norm_out, l1_diff_w_out, l1_norm_ffn2, l1_ffn2_w_in, l1_ffn2_w_down, final_norm):
    bsz, t, d = x.shape
    m = bsz * t
    lambda_init = 0.8 - 0.6 * math.exp(-0.3 * 1)

    h = _ffn(x.reshape(m, d), l0_norm_ffn1, l0_ffn1_w_in, l0_ffn1_w_down, final_norm, False)
    h = _gla_layer(h, l0_norm_mix, l0_gla_w_in, l0_gla_w_gate2, l0_gla_b_gate2,
                   l0_gla_norm_out, l0_gla_w_out, t)
    h = _ffn(h, l0_norm_ffn2, l0_ffn2_w_in, l0_ffn2_w_down, final_norm, False)

    h = _ffn(h, l1_norm_ffn1, l1_ffn1_w_in, l1_ffn1_w_down, final_norm, False)
    qt, k, vt = _qkv_proj(h.reshape(bsz, t, d), l1_norm_mix, l1_diff_w_in)
    o = _diff_attn(qt, k, vt, l1_diff_lambda_q1, l1_diff_lambda_k1, l1_diff_lambda_q2, l1_diff_lambda_k2,
                   l1_diff_norm_out, lambda_init)
    h = _ffn(h, l1_norm_ffn2, l1_ffn2_w_in, l1_ffn2_w_down, final_norm, True,
             mixer_out=o.reshape(m, d), w_out=l1_diff_w_out)
    return h.reshape(bsz, t, d)
```

```python
import functools
import math

import jax
import jax.numpy as jnp
from jax import lax
from jax.experimental import pallas as pl
from jax.experimental.pallas import tpu as pltpu

F32 = jnp.float32
BF16 = jnp.bfloat16

D_MODEL = 1024
D_FF = 2816
NORM_EPS = 1e-6

GLA_HEADS = 4
GLA_DK = 512
GLA_DV = 1024
GLA_HEAD_K = 128
GLA_HEAD_V = 256
GLA_GATE_RANK = 16
GLA_TAU = 16.0
GLA_CHUNK = 64
GLA_MAIN = 2 * GLA_DK + 2 * GLA_DV

DIFF_HEADS = 8
DIFF_HEAD_DIM = 64
DIFF_V_DIM = 128

LOG2_E = math.log2(math.e)
LANES = 128
BF16_ROWS = 16
VMEM_LIMIT = 56 * 1024 * 1024

FFN_TM = 1024
FFN_TM_FUSED = 512
FFN_TF = 256
GLA_PIECE = 512
GLA_TM = 512
PROJ_TM = 512
ATT_TQ = 512
ATT_TK = 512
ATT_G = 4
ATT_SPLIT = 1
ATT_SKEW_SM = 1
ATT_SKEW_PV = 1

NT_DIMS = (((1,), (1,)), ((), ()))
TN_DIMS = (((0,), (0,)), ((), ()))


def _rms(x, g):
    ms = jnp.mean(x * x, axis=-1, keepdims=True)
    return (x * lax.rsqrt(ms + NORM_EPS)) * g


def _silu(x):
    return x * (1.0 / (1.0 + jnp.exp(-x)))


def _resident(shape):
    nd = len(shape)
    return pl.BlockSpec(shape, lambda *_: (0,) * nd, pipeline_mode=pl.Buffered(1))


def _params(sem):
    return pltpu.CompilerParams(dimension_semantics=sem, vmem_limit_bytes=VMEM_LIMIT)


def _ffn_body(*refs, final_norm, fused_proj):
    if fused_proj:
        x_ref, o_ref, wout_ref, g_ref, win_ref, wdown_ref, gfin_ref, y_ref = refs
        x = x_ref[...] + jnp.dot(o_ref[...], wout_ref[...].astype(BF16), preferred_element_type=F32)
    else:
        x_ref, g_ref, win_ref, wdown_ref, gfin_ref, y_ref = refs
        x = x_ref[...]
    xn = _rms(x, g_ref[...]).astype(BF16)
    acc = None
    for c in range(D_FF // FFN_TF):
        lo = c * FFN_TF
        w_gate = win_ref[:, lo:lo + FFN_TF].astype(BF16)
        w_up = win_ref[:, D_FF + lo:D_FF + lo + FFN_TF].astype(BF16)
        gate = jnp.dot(xn, w_gate, preferred_element_type=F32)
        up = jnp.dot(xn, w_up, preferred_element_type=F32)
        h = (_silu(gate) * up).astype(BF16)
        part = jnp.dot(h, wdown_ref[lo:lo + FFN_TF, :].astype(BF16), preferred_element_type=F32)
        acc = part if acc is None else acc + part
    y = x + 0.5 * acc
    if final_norm:
        y = _rms(y, gfin_ref[...])
    y_ref[...] = y


def _ffn(x2d, g, w_in, w_down, g_final, final_norm, mixer_out=None, w_out=None):
    m = x2d.shape[0]
    fused = mixer_out is not None
    tm = FFN_TM_FUSED if fused else FFN_TM
    row = pl.BlockSpec((tm, D_MODEL), lambda i: (i, 0))
    args, specs = [x2d], [row]
    if fused:
        args += [mixer_out, w_out]
        specs += [row, _resident((D_MODEL, D_MODEL))]
    args += [g.reshape(1, -1), w_in, w_down, g_final.reshape(1, -1)]
    specs += [_resident((1, D_MODEL)), _resident((D_MODEL, 2 * D_FF)), _resident((D_FF, D_MODEL)),
              _resident((1, D_MODEL))]
    return pl.pallas_call(
        functools.partial(_ffn_body, final_norm=final_norm, fused_proj=fused),
        grid=(m // tm,),
        in_specs=specs,
        out_specs=row,
        out_shape=jax.ShapeDtypeStruct((m, D_MODEL), F32),
        compiler_params=_params(("arbitrary",)),
        name="ffn",
    )(*args)


def _gla_body(xn_ref, xp_ref, gmix_ref, win_ref, wg1_ref, wg2_ref, bg2_ref, gout_ref, wout_ref,
              o_ref, proj_ref, la_ref, state_ref, og_ref, *, tiles_per_seq):
    g = pl.program_id(0)
    nxt = g % 2
    prv = 1 - nxt

    @pl.when(g == 0)
    def _():
        proj_ref[...] = jnp.zeros_like(proj_ref)
        la_ref[...] = jnp.zeros_like(la_ref)

    @pl.when((g == 0) | ((g - 1) % tiles_per_seq == 0))
    def _():
        state_ref[...] = jnp.zeros_like(state_ref)

    hn = _rms(xn_ref[...], gmix_ref[...]).astype(BF16)

    rows = lax.broadcasted_iota(jnp.int32, (GLA_CHUNK, GLA_CHUNK), 0)
    cols = lax.broadcasted_iota(jnp.int32, (GLA_CHUNK, GLA_CHUNK), 1)
    causal = rows >= cols
    tril = causal.astype(BF16)
    scale = GLA_HEAD_K ** -0.5
    n_chunks = GLA_TM // GLA_CHUNK

    def project(piece):
        if piece < GLA_MAIN // GLA_PIECE:
            c0 = piece * GLA_PIECE
            proj_ref[nxt, :, c0:c0 + GLA_PIECE] = jnp.dot(hn, win_ref[:, c0:c0 + GLA_PIECE],
                                                          preferred_element_type=F32)
        else:
            g_lr = jnp.dot(hn, wg1_ref[...], preferred_element_type=F32)
            z = jnp.dot(g_lr.astype(BF16), wg2_ref[...], preferred_element_type=F32) + bg2_ref[...]
            la_ref[nxt] = (jnp.minimum(z, 0.0) - jnp.log(1.0 + jnp.exp(-jnp.abs(z)))) * (1.0 / GLA_TAU)

    def rows_of(n):
        return slice(n * GLA_CHUNK, (n + 1) * GLA_CHUNK)

    def ks(h):
        return slice(h * GLA_HEAD_K, (h + 1) * GLA_HEAD_K)

    def vs(h):
        return slice(h * GLA_HEAD_V, (h + 1) * GLA_HEAD_V)

    chunk_heads = [(n, h) for n in range(n_chunks) for h in range(GLA_HEADS)]
    project(0)
    cum = []
    for n in range(n_chunks):
        la = la_ref[prv, rows_of(n), :]
        la_hi = la.astype(BF16)
        la_r = la - la_hi.astype(F32)
        la_mid = la_r.astype(BF16)
        la_lo = (la_r - la_mid.astype(F32)).astype(BF16)
        cum.append(jnp.dot(tril, la_hi, preferred_element_type=F32)
                   + jnp.dot(tril, la_mid, preferred_element_type=F32)
                   + jnp.dot(tril, la_lo, preferred_element_type=F32))

    project(1)
    q_dec, k_inv, k_tail, decay, v = [], [], [], [], []
    for n in range(n_chunks):
        b = cum[n]
        b_last = b[GLA_CHUNK - 1:GLA_CHUNK]
        q = proj_ref[prv, rows_of(n), 0:GLA_DK]
        k = proj_ref[prv, rows_of(n), GLA_DK:2 * GLA_DK]
        q_dec.append(((q * scale) * jnp.exp(b)).astype(BF16))
        k_inv.append((k * jnp.exp(-b)).astype(BF16))
        k_tail.append((k * jnp.exp(b_last - b)).astype(BF16))
        decay.append(jnp.exp(b_last))
        v.append(proj_ref[prv, rows_of(n), 2 * GLA_DK:2 * GLA_DK + GLA_DV].astype(BF16))

    project(2)
    att = {}
    for n, h in chunk_heads:
        a = lax.dot_general(q_dec[n][:, ks(h)], k_inv[n][:, ks(h)], NT_DIMS, preferred_element_type=F32)
        att[n, h] = jnp.where(causal, a, 0.0).astype(BF16)

    project(3)
    upd = {}
    for n, h in chunk_heads:
        upd[n, h] = lax.dot_general(v[n][:, vs(h)], k_tail[n][:, ks(h)], TN_DIMS,
                                    preferred_element_type=F32)

    project(4)
    start = {}
    for h in range(GLA_HEADS):
        s_t = state_ref[h]
        for n in range(n_chunks):
            start[n, h] = s_t.astype(BF16)
            s_t = s_t * decay[n][:, ks(h)] + upd[n, h]
        state_ref[h] = s_t

    project(5)
    outs = {}
    for n, h in chunk_heads:
        o = jnp.dot(att[n, h], v[n][:, vs(h)], preferred_element_type=F32)
        outs[n, h] = o + lax.dot_general(q_dec[n][:, ks(h)], start[n, h], NT_DIMS, preferred_element_type=F32)

    project(6)
    for n, h in chunk_heads:
        r0 = 2 * GLA_DK + GLA_DV + h * GLA_HEAD_V
        r = proj_ref[prv, rows_of(n), r0:r0 + GLA_HEAD_V]
        o = _rms(outs[n, h], gout_ref[...])
        og_ref[rows_of(n), vs(h)] = (o * _silu(r)).astype(BF16)

    o_ref[...] = xp_ref[...] + jnp.dot(og_ref[...], wout_ref[...], preferred_element_type=F32)


def _gla_layer(x2d, g_mix, w_in, w_gate2, b_gate2, g_out, w_out, seq_len):
    m = x2d.shape[0]
    n_tiles = m // GLA_TM
    w_main = w_in[:, :GLA_MAIN].astype(BF16)
    w_g1 = jnp.pad(w_in[:, GLA_MAIN:], ((0, 0), (0, LANES - GLA_GATE_RANK))).astype(BF16)
    w_g2 = jnp.pad(w_gate2, ((0, LANES - GLA_GATE_RANK), (0, 0))).astype(BF16)
    nxt_spec = pl.BlockSpec((GLA_TM, D_MODEL), lambda g: (jnp.minimum(g, n_tiles - 1), 0))
    prv_spec = pl.BlockSpec((GLA_TM, D_MODEL), lambda g: (jnp.maximum(g - 1, 0), 0))
    return pl.pallas_call(
        functools.partial(_gla_body, tiles_per_seq=seq_len // GLA_TM),
        grid=(n_tiles + 1,),
        in_specs=[nxt_spec, prv_spec, _resident((1, D_MODEL)), _resident((D_MODEL, GLA_MAIN)),
                  _resident((D_MODEL, LANES)), _resident((LANES, GLA_DK)), _resident((1, GLA_DK)),
                  _resident((1, GLA_HEAD_V)), _resident((GLA_DV, D_MODEL))],
        out_specs=prv_spec,
        out_shape=jax.ShapeDtypeStruct((m, D_MODEL), F32),
        scratch_shapes=[pltpu.VMEM((2, GLA_TM, GLA_MAIN), F32),
                        pltpu.VMEM((2, GLA_TM, GLA_DK), F32),
                        pltpu.VMEM((GLA_HEADS, GLA_HEAD_V, GLA_HEAD_K), F32),
                        pltpu.VMEM((GLA_TM, GLA_DV), BF16)],
        compiler_params=_params(("arbitrary",)),
        name="gla_layer",
    )(x2d, x2d, g_mix.reshape(1, -1), w_main, w_g1, w_g2, b_gate2.reshape(1, -1),
      g_out.reshape(1, -1), w_out.astype(BF16))


def _qkv_body(x_ref, g_ref, wqt_ref, wk_ref, wvt_ref, qt_ref, k_ref, vt_ref):
    hn = _rms(x_ref[0], g_ref[...]).astype(BF16)
    qt = lax.dot_general(wqt_ref[...], hn, NT_DIMS, preferred_element_type=F32)
    qt_ref[0] = (qt * (DIFF_HEAD_DIM ** -0.5 * LOG2_E)).astype(BF16)
    k_ref[0] = jnp.dot(hn, wk_ref[...], preferred_element_type=F32).astype(BF16)
    vt = lax.dot_general(wvt_ref[...], hn, NT_DIMS, preferred_element_type=F32)
    for c in range(PROJ_TM // ATT_TK):
        vt_ref[0, c] = vt[:, c * ATT_TK:(c + 1) * ATT_TK].astype(BF16)


def _qkv_proj(x, g, w_in):
    bsz, t, _ = x.shape
    w_q, w_k, w_v = jnp.split(w_in.astype(BF16), 3, axis=1)
    return pl.pallas_call(
        _qkv_body,
        grid=(bsz, t // PROJ_TM),
        in_specs=[pl.BlockSpec((1, PROJ_TM, D_MODEL), lambda b, i: (b, i, 0)),
                  _resident((1, D_MODEL)), _resident((D_MODEL, D_MODEL)),
                  _resident((D_MODEL, D_MODEL)), _resident((D_MODEL, D_MODEL))],
        out_specs=[pl.BlockSpec((1, D_MODEL, PROJ_TM), lambda b, i: (b, 0, i)),
                   pl.BlockSpec((1, PROJ_TM, D_MODEL), lambda b, i: (b, i, 0)),
                   pl.BlockSpec((1, PROJ_TM // ATT_TK, D_MODEL, ATT_TK), lambda b, i: (b, i, 0, 0))],
        out_shape=[jax.ShapeDtypeStruct((bsz, D_MODEL, t), BF16),
                   jax.ShapeDtypeStruct((bsz, t, D_MODEL), BF16),
                   jax.ShapeDtypeStruct((bsz, t // ATT_TK, D_MODEL, ATT_TK), BF16)],
        compiler_params=_params(("arbitrary", "arbitrary")),
        name="diff_qkv",
    )(x, g.reshape(1, -1), w_q.T, w_k, w_v.T)


def _diff_attn_body(qt_ref, k_ref, vt_ref, lq1_ref, lk1_ref, lq2_ref, lk2_ref, gout_ref, o_ref, acc_ref,
                    *, lambda_init):
    i = pl.program_id(2)
    m2 = 2 * ATT_TQ
    feat = lax.broadcasted_iota(jnp.int32, (DIFF_V_DIM, ATT_TQ), 0)
    q2t = []
    for g in range(ATT_G):
        qt = qt_ref[0, g * DIFF_V_DIM:(g + 1) * DIFF_V_DIM, :].astype(F32)
        q2t.append(jnp.concatenate([jnp.where(feat < DIFF_HEAD_DIM, qt, 0.0),
                                    jnp.where(feat >= DIFF_HEAD_DIM, qt, 0.0)], axis=1).astype(BF16))

    heads = [slice(g * DIFF_V_DIM, (g + 1) * DIFF_V_DIM) for g in range(ATT_G)]

    unit_w = m2 // ATT_SPLIT
    units = [(g, slice(c * unit_w, (c + 1) * unit_w)) for g in range(ATT_G) for c in range(ATT_SPLIT)]

    def step(j, m_run, diag_offset=None):
        scores, probs = {}, {}

        def score(u):
            g, lanes = units[u]
            k = k_ref[0, pl.ds(j * ATT_TK, ATT_TK), heads[g]]
            s = jnp.dot(k, q2t[g][:, lanes], preferred_element_type=F32)
            if diag_offset is not None:
                kpos = lax.broadcasted_iota(jnp.int32, s.shape, 0) + diag_offset
                qpos = (lax.broadcasted_iota(jnp.int32, s.shape, 1) + lanes.start) % ATT_TQ
                s = jnp.where(qpos >= kpos, s, -jnp.inf)
            scores[u] = s

        def softmax(u):
            m_new = jnp.maximum(m_run[u], jnp.max(scores[u], axis=0, keepdims=True))
            probs[u] = (m_new, jnp.exp2(m_run[u] - m_new), jnp.exp2(scores[u] - m_new).astype(BF16))

        def value(u):
            g, lanes = units[u]
            _, alpha, p = probs[u]
            vt1 = jnp.concatenate([vt_ref[0, j, heads[g], :], ones_rows], axis=0)
            acc_ref[g, :, lanes] = (alpha * acc_ref[g, :, lanes]
                                    + jnp.dot(vt1, p, preferred_element_type=F32))

        for t in range(len(units) + ATT_SKEW_PV):
            if t < len(units):
                score(t)
            if 0 <= t - ATT_SKEW_SM < len(units):
                softmax(t - ATT_SKEW_SM)
            if 0 <= t - ATT_SKEW_PV < len(units):
                value(t - ATT_SKEW_PV)
        return tuple(probs[u][0] for u in range(len(units)))

    ones_rows = jnp.ones((BF16_ROWS, ATT_TK), BF16)
    acc_ref[...] = jnp.zeros_like(acc_ref)
    blocks_per_tile = ATT_TQ // ATT_TK
    m_run = lax.fori_loop(0, i * blocks_per_tile, step,
                          tuple(jnp.full((1, unit_w), -jnp.inf, F32) for _ in units))
    for d in range(blocks_per_tile):
        m_run = step(i * blocks_per_tile + d, m_run, diag_offset=d * ATT_TK)

    lam = (jnp.exp(jnp.sum(lq1_ref[...] * lk1_ref[...])) - jnp.exp(jnp.sum(lq2_ref[...] * lk2_ref[...]))
           + lambda_init)
    for g in range(ATT_G):
        ot = acc_ref[g, :DIFF_V_DIM, :] / acc_ref[g, DIFF_V_DIM:DIFF_V_DIM + 1, :]
        o = (ot[:, :ATT_TQ] - lam * ot[:, ATT_TQ:]).T
        o = _rms(o, gout_ref[...]) * (1.0 - lambda_init)
        o_ref[0, :, g * DIFF_V_DIM:(g + 1) * DIFF_V_DIM] = o.astype(BF16)


def _diff_attn(qt, k, vt, lq1, lk1, lq2, lk2, g_out, lambda_init):
    bsz, t, _ = k.shape
    gw = ATT_G * DIFF_V_DIM
    lam_spec = _resident((1, DIFF_HEAD_DIM))
    return pl.pallas_call(
        functools.partial(_diff_attn_body, lambda_init=lambda_init),
        grid=(bsz, DIFF_HEADS // ATT_G, t // ATT_TQ),
        in_specs=[pl.BlockSpec((1, gw, ATT_TQ), lambda b, h, i: (b, h, i)),
                  pl.BlockSpec((1, t, gw), lambda b, h, i: (b, 0, h)),
                  pl.BlockSpec((1, t // ATT_TK, gw, ATT_TK), lambda b, h, i: (b, 0, h, 0)),
                  lam_spec, lam_spec, lam_spec, lam_spec, _resident((1, DIFF_V_DIM))],
        out_specs=pl.BlockSpec((1, ATT_TQ, gw), lambda b, h, i: (b, i, h)),
        out_shape=jax.ShapeDtypeStruct(k.shape, BF16),
        scratch_shapes=[pltpu.VMEM((ATT_G, DIFF_V_DIM + BF16_ROWS, 2 * ATT_TQ), F32)],
        compiler_params=_params(("arbitrary", "arbitrary", "arbitrary")),
        name="diff_attn",
    )(qt, k, vt, lq1.reshape(1, -1), lk1.reshape(1, -1), lq2.reshape(1, -1), lk2.reshape(1, -1),
      g_out.reshape(1, -1))


def kernel(x, l0_norm_ffn1, l0_ffn1_w_in, l0_ffn1_w_down, l0_norm_mix, l0_gla_w_in, l0_gla_w_gate2, l0_gla_b_gate2, l0_gla_norm_out, l0_gla_w_out, l0_norm_ffn2, l0_ffn2_w_in, l0_ffn2_w_down, l1_norm_ffn1, l1_ffn1_w_in, l1_ffn1_w_down, l1_norm_mix, l1_diff_w_in, l1_diff_lambda_q1, l1_diff_lambda_k1, l1_diff_lambda_q2, l1_diff_lambda_k2, l1_diff_norm_out, l1_diff_w_out, l1_norm_ffn2, l1_ffn2_w_in, l1_ffn2_w_down, final_norm):
    bsz, t, d = x.shape
    m = bsz * t
    lambda_init = 0.8 - 0.6 * math.exp(-0.3 * 1)

    h = _ffn(x.reshape(m, d), l0_norm_ffn1, l0_ffn1_w_in, l0_ffn1_w_down, final_norm, False)
    h = _gla_layer(h, l0_norm_mix, l0_gla_w_in, l0_gla_w_gate2, l0_gla_b_gate2,
                   l0_gla_norm_out, l0_gla_w_out, t)
    h = _ffn(h, l0_norm_ffn2, l0_ffn2_w_in, l0_ffn2_w_down, final_norm, False)

    h = _ffn(h, l1_norm_ffn1, l1_ffn1_w_in, l1_ffn1_w_down, final_norm, False)
    qt, k, vt = _qkv_proj(h.reshape(bsz, t, d), l1_norm_mix, l1_diff_w_in)
    o = _diff_attn(qt, k, vt, l1_diff_lambda_q1, l1_diff_lambda_k1, l1_diff_lambda_q2, l1_diff_lambda_k2,
                   l1_diff_norm_out, lambda_init)
    h = _ffn(h, l1_norm_ffn2, l1_ffn2_w_in, l1_ffn2_w_down, final_norm, True,
             mixer_out=o.reshape(m, d), w_out=l1_diff_w_out)
    return h.reshape(bsz, t, d)
```

```python
import functools
import math

import jax
import jax.numpy as jnp
from jax import lax
from jax.experimental import pallas as pl
from jax.experimental.pallas import tpu as pltpu

F32 = jnp.float32
BF16 = jnp.bfloat16

D_MODEL = 1024
D_FF = 2816
NORM_EPS = 1e-6

GLA_HEADS = 4
GLA_DK = 512
GLA_DV = 1024
GLA_HEAD_K = 128
GLA_HEAD_V = 256
GLA_GATE_RANK = 16
GLA_TAU = 16.0
GLA_CHUNK = 64
GLA_MAIN = 2 * GLA_DK + 2 * GLA_DV

DIFF_HEADS = 8
DIFF_HEAD_DIM = 64
DIFF_V_DIM = 128

LOG2_E = math.log2(math.e)
LANES = 128
BF16_ROWS = 16
VMEM_LIMIT = 56 * 1024 * 1024

FFN_TM = 1024
FFN_TM_FUSED = 512
FFN_TF = 256
GLA_PIECE = 512
GLA_TM = 512
PROJ_TM = 1024
ATT_TQ = 512
ATT_TK = 512
ATT_G = 4
ATT_SPLIT = 1
ATT_SKEW_SM = 1
ATT_SKEW_PV = 1

NT_DIMS = (((1,), (1,)), ((), ()))
TN_DIMS = (((0,), (0,)), ((), ()))


def _rms(x, g):
    ms = jnp.mean(x * x, axis=-1, keepdims=True)
    return (x * lax.rsqrt(ms + NORM_EPS)) * g


def _silu(x):
    return x * (1.0 / (1.0 + jnp.exp(-x)))


def _resident(shape):
    nd = len(shape)
    return pl.BlockSpec(shape, lambda *_: (0,) * nd, pipeline_mode=pl.Buffered(1))


def _params(sem):
    return pltpu.CompilerParams(dimension_semantics=sem, vmem_limit_bytes=VMEM_LIMIT)


def _ffn_body(*refs, final_norm, fused_proj):
    if fused_proj:
        x_ref, o_ref, wout_ref, g_ref, win_ref, wdown_ref, gfin_ref, y_ref = refs
        x = x_ref[...] + jnp.dot(o_ref[...], wout_ref[...].astype(BF16), preferred_element_type=F32)
    else:
        x_ref, g_ref, win_ref, wdown_ref, gfin_ref, y_ref = refs
        x = x_ref[...]
    xn = _rms(x, g_ref[...]).astype(BF16)
    acc = None
    for c in range(D_FF // FFN_TF):
        lo = c * FFN_TF
        w_gate = win_ref[:, lo:lo + FFN_TF].astype(BF16)
        w_up = win_ref[:, D_FF + lo:D_FF + lo + FFN_TF].astype(BF16)
        gate = jnp.dot(xn, w_gate, preferred_element_type=F32)
        up = jnp.dot(xn, w_up, preferred_element_type=F32)
        h = (_silu(gate) * up).astype(BF16)
        part = jnp.dot(h, wdown_ref[lo:lo + FFN_TF, :].astype(BF16), preferred_element_type=F32)
        acc = part if acc is None else acc + part
    y = x + 0.5 * acc
    if final_norm:
        y = _rms(y, gfin_ref[...])
    y_ref[...] = y


def _ffn(x2d, g, w_in, w_down, g_final, final_norm, mixer_out=None, w_out=None):
    m = x2d.shape[0]
    fused = mixer_out is not None
    tm = FFN_TM_FUSED if fused else FFN_TM
    row = pl.BlockSpec((tm, D_MODEL), lambda i: (i, 0))
    args, specs = [x2d], [row]
    if fused:
        args += [mixer_out, w_out]
        specs += [row, _resident((D_MODEL, D_MODEL))]
    args += [g.reshape(1, -1), w_in, w_down, g_final.reshape(1, -1)]
    specs += [_resident((1, D_MODEL)), _resident((D_MODEL, 2 * D_FF)), _resident((D_FF, D_MODEL)),
              _resident((1, D_MODEL))]
    return pl.pallas_call(
        functools.partial(_ffn_body, final_norm=final_norm, fused_proj=fused),
        grid=(m // tm,),
        in_specs=specs,
        out_specs=row,
        out_shape=jax.ShapeDtypeStruct((m, D_MODEL), F32),
        compiler_params=_params(("arbitrary",)),
        name="ffn",
    )(*args)


def _gla_body(xn_ref, xp_ref, gmix_ref, win_ref, wg1_ref, wg2_ref, bg2_ref, gout_ref, wout_ref,
              o_ref, proj_ref, la_ref, state_ref, og_ref, *, tiles_per_seq):
    g = pl.program_id(0)
    nxt = g % 2
    prv = 1 - nxt

    @pl.when(g == 0)
    def _():
        proj_ref[...] = jnp.zeros_like(proj_ref)
        la_ref[...] = jnp.zeros_like(la_ref)

    @pl.when((g == 0) | ((g - 1) % tiles_per_seq == 0))
    def _():
        state_ref[...] = jnp.zeros_like(state_ref)

    hn = _rms(xn_ref[...], gmix_ref[...]).astype(BF16)

    rows = lax.broadcasted_iota(jnp.int32, (GLA_CHUNK, GLA_CHUNK), 0)
    cols = lax.broadcasted_iota(jnp.int32, (GLA_CHUNK, GLA_CHUNK), 1)
    causal = rows >= cols
    tril = causal.astype(BF16)
    scale = GLA_HEAD_K ** -0.5
    n_chunks = GLA_TM // GLA_CHUNK

    def project(piece):
        if piece < GLA_MAIN // GLA_PIECE:
            c0 = piece * GLA_PIECE
            proj_ref[nxt, :, c0:c0 + GLA_PIECE] = jnp.dot(hn, win_ref[:, c0:c0 + GLA_PIECE],
                                                          preferred_element_type=F32)
        else:
            g_lr = jnp.dot(hn, wg1_ref[...], preferred_element_type=F32)
            z = jnp.dot(g_lr.astype(BF16), wg2_ref[...], preferred_element_type=F32) + bg2_ref[...]
            la_ref[nxt] = (jnp.minimum(z, 0.0) - jnp.log(1.0 + jnp.exp(-jnp.abs(z)))) * (1.0 / GLA_TAU)

    def rows_of(n):
        return slice(n * GLA_CHUNK, (n + 1) * GLA_CHUNK)

    def ks(h):
        return slice(h * GLA_HEAD_K, (h + 1) * GLA_HEAD_K)

    def vs(h):
        return slice(h * GLA_HEAD_V, (h + 1) * GLA_HEAD_V)

    chunk_heads = [(n, h) for n in range(n_chunks) for h in range(GLA_HEADS)]
    project(0)
    cum = []
    for n in range(n_chunks):
        la = la_ref[prv, rows_of(n), :]
        la_hi = la.astype(BF16)
        la_r = la - la_hi.astype(F32)
        la_mid = la_r.astype(BF16)
        la_lo = (la_r - la_mid.astype(F32)).astype(BF16)
        cum.append(jnp.dot(tril, la_hi, preferred_element_type=F32)
                   + jnp.dot(tril, la_mid, preferred_element_type=F32)
                   + jnp.dot(tril, la_lo, preferred_element_type=F32))

    project(1)
    q_dec, k_inv, k_tail, decay, v = [], [], [], [], []
    for n in range(n_chunks):
        b = cum[n]
        b_last = b[GLA_CHUNK - 1:GLA_CHUNK]
        q = proj_ref[prv, rows_of(n), 0:GLA_DK]
        k = proj_ref[prv, rows_of(n), GLA_DK:2 * GLA_DK]
        q_dec.append(((q * scale) * jnp.exp(b)).astype(BF16))
        k_inv.append((k * jnp.exp(-b)).astype(BF16))
        k_tail.append((k * jnp.exp(b_last - b)).astype(BF16))
        decay.append(jnp.exp(b_last))
        v.append(proj_ref[prv, rows_of(n), 2 * GLA_DK:2 * GLA_DK + GLA_DV].astype(BF16))

    project(2)
    att = {}
    for n, h in chunk_heads:
        a = lax.dot_general(q_dec[n][:, ks(h)], k_inv[n][:, ks(h)], NT_DIMS, preferred_element_type=F32)
        att[n, h] = jnp.where(causal, a, 0.0).astype(BF16)

    project(3)
    upd = {}
    for n, h in chunk_heads:
        upd[n, h] = lax.dot_general(v[n][:, vs(h)], k_tail[n][:, ks(h)], TN_DIMS,
                                    preferred_element_type=F32)

    project(4)
    start = {}
    for h in range(GLA_HEADS):
        s_t = state_ref[h]
        for n in range(n_chunks):
            start[n, h] = s_t.astype(BF16)
            s_t = s_t * decay[n][:, ks(h)] + upd[n, h]
        state_ref[h] = s_t

    project(5)
    outs = {}
    for n, h in chunk_heads:
        o = jnp.dot(att[n, h], v[n][:, vs(h)], preferred_element_type=F32)
        outs[n, h] = o + lax.dot_general(q_dec[n][:, ks(h)], start[n, h], NT_DIMS, preferred_element_type=F32)

    project(6)
    for n, h in chunk_heads:
        r0 = 2 * GLA_DK + GLA_DV + h * GLA_HEAD_V
        r = proj_ref[prv, rows_of(n), r0:r0 + GLA_HEAD_V]
        o = _rms(outs[n, h], gout_ref[...])
        og_ref[rows_of(n), vs(h)] = (o * _silu(r)).astype(BF16)

    o_ref[...] = xp_ref[...] + jnp.dot(og_ref[...], wout_ref[...], preferred_element_type=F32)


def _gla_layer(x2d, g_mix, w_in, w_gate2, b_gate2, g_out, w_out, seq_len):
    m = x2d.shape[0]
    n_tiles = m // GLA_TM
    w_main = w_in[:, :GLA_MAIN].astype(BF16)
    w_g1 = jnp.pad(w_in[:, GLA_MAIN:], ((0, 0), (0, LANES - GLA_GATE_RANK))).astype(BF16)
    w_g2 = jnp.pad(w_gate2, ((0, LANES - GLA_GATE_RANK), (0, 0))).astype(BF16)
    nxt_spec = pl.BlockSpec((GLA_TM, D_MODEL), lambda g: (jnp.minimum(g, n_tiles - 1), 0))
    prv_spec = pl.BlockSpec((GLA_TM, D_MODEL), lambda g: (jnp.maximum(g - 1, 0), 0))
    return pl.pallas_call(
        functools.partial(_gla_body, tiles_per_seq=seq_len // GLA_TM),
        grid=(n_tiles + 1,),
        in_specs=[nxt_spec, prv_spec, _resident((1, D_MODEL)), _resident((D_MODEL, GLA_MAIN)),
                  _resident((D_MODEL, LANES)), _resident((LANES, GLA_DK)), _resident((1, GLA_DK)),
                  _resident((1, GLA_HEAD_V)), _resident((GLA_DV, D_MODEL))],
        out_specs=prv_spec,
        out_shape=jax.ShapeDtypeStruct((m, D_MODEL), F32),
        scratch_shapes=[pltpu.VMEM((2, GLA_TM, GLA_MAIN), F32),
                        pltpu.VMEM((2, GLA_TM, GLA_DK), F32),
                        pltpu.VMEM((GLA_HEADS, GLA_HEAD_V, GLA_HEAD_K), F32),
                        pltpu.VMEM((GLA_TM, GLA_DV), BF16)],
        compiler_params=_params(("arbitrary",)),
        name="gla_layer",
    )(x2d, x2d, g_mix.reshape(1, -1), w_main, w_g1, w_g2, b_gate2.reshape(1, -1),
      g_out.reshape(1, -1), w_out.astype(BF16))


def _qkv_body(x_ref, g_ref, wqt_ref, wk_ref, wvt_ref, qt_ref, k_ref, vt_ref):
    hn = _rms(x_ref[0], g_ref[...]).astype(BF16)
    qt = lax.dot_general(wqt_ref[...], hn, NT_DIMS, preferred_element_type=F32)
    qt_ref[0] = (qt * (DIFF_HEAD_DIM ** -0.5 * LOG2_E)).astype(BF16)
    k_ref[0] = jnp.dot(hn, wk_ref[...], preferred_element_type=F32).astype(BF16)
    vt = lax.dot_general(wvt_ref[...], hn, NT_DIMS, preferred_element_type=F32)
    for c in range(PROJ_TM // ATT_TK):
        vt_ref[0, c] = vt[:, c * ATT_TK:(c + 1) * ATT_TK].astype(BF16)


def _qkv_proj(x, g, w_in):
    bsz, t, _ = x.shape
    w_q, w_k, w_v = jnp.split(w_in.astype(BF16), 3, axis=1)
    return pl.pallas_call(
        _qkv_body,
        grid=(bsz, t // PROJ_TM),
        in_specs=[pl.BlockSpec((1, PROJ_TM, D_MODEL), lambda b, i: (b, i, 0)),
                  _resident((1, D_MODEL)), _resident((D_MODEL, D_MODEL)),
                  _resident((D_MODEL, D_MODEL)), _resident((D_MODEL, D_MODEL))],
        out_specs=[pl.BlockSpec((1, D_MODEL, PROJ_TM), lambda b, i: (b, 0, i)),
                   pl.BlockSpec((1, PROJ_TM, D_MODEL), lambda b, i: (b, i, 0)),
                   pl.BlockSpec((1, PROJ_TM // ATT_TK, D_MODEL, ATT_TK), lambda b, i: (b, i, 0, 0))],
        out_shape=[jax.ShapeDtypeStruct((bsz, D_MODEL, t), BF16),
                   jax.ShapeDtypeStruct((bsz, t, D_MODEL), BF16),
                   jax.ShapeDtypeStruct((bsz, t // ATT_TK, D_MODEL, ATT_TK), BF16)],
        compiler_params=_params(("arbitrary", "arbitrary")),
        name="diff_qkv",
    )(x, g.reshape(1, -1), w_q.T, w_k, w_v.T)


def _diff_attn_body(qt_ref, k_ref, vt_ref, lq1_ref, lk1_ref, lq2_ref, lk2_ref, gout_ref, o_ref, acc_ref,
                    *, lambda_init):
    i = pl.program_id(2)
    m2 = 2 * ATT_TQ
    feat = lax.broadcasted_iota(jnp.int32, (DIFF_V_DIM, ATT_TQ), 0)
    q2t = []
    for g in range(ATT_G):
        qt = qt_ref[0, g * DIFF_V_DIM:(g + 1) * DIFF_V_DIM, :].astype(F32)
        q2t.append(jnp.concatenate([jnp.where(feat < DIFF_HEAD_DIM, qt, 0.0),
                                    jnp.where(feat >= DIFF_HEAD_DIM, qt, 0.0)], axis=1).astype(BF16))

    heads = [slice(g * DIFF_V_DIM, (g + 1) * DIFF_V_DIM) for g in range(ATT_G)]

    unit_w = m2 // ATT_SPLIT
    units = [(g, slice(c * unit_w, (c + 1) * unit_w)) for g in range(ATT_G) for c in range(ATT_SPLIT)]

    def step(j, m_run, diag_offset=None):
        scores, probs = {}, {}

        def score(u):
            g, lanes = units[u]
            k = k_ref[0, pl.ds(j * ATT_TK, ATT_TK), heads[g]]
            s = jnp.dot(k, q2t[g][:, lanes], preferred_element_type=F32)
            if diag_offset is not None:
                kpos = lax.broadcasted_iota(jnp.int32, s.shape, 0) + diag_offset
                qpos = (lax.broadcasted_iota(jnp.int32, s.shape, 1) + lanes.start) % ATT_TQ
                s = jnp.where(qpos >= kpos, s, -jnp.inf)
            scores[u] = s

        def softmax(u):
            m_new = jnp.maximum(m_run[u], jnp.max(scores[u], axis=0, keepdims=True))
            probs[u] = (m_new, jnp.exp2(m_run[u] - m_new), jnp.exp2(scores[u] - m_new).astype(BF16))

        def value(u):
            g, lanes = units[u]
            _, alpha, p = probs[u]
            vt1 = jnp.concatenate([vt_ref[0, j, heads[g], :], ones_rows], axis=0)
            acc_ref[g, :, lanes] = (alpha * acc_ref[g, :, lanes]
                                    + jnp.dot(vt1, p, preferred_element_type=F32))

        for t in range(len(units) + ATT_SKEW_PV):
            if t < len(units):
                score(t)
            if 0 <= t - ATT_SKEW_SM < len(units):
                softmax(t - ATT_SKEW_SM)
            if 0 <= t - ATT_SKEW_PV < len(units):
                value(t - ATT_SKEW_PV)
        return tuple(probs[u][0] for u in range(len(units)))

    ones_rows = jnp.ones((BF16_ROWS, ATT_TK), BF16)
    acc_ref[...] = jnp.zeros_like(acc_ref)
    blocks_per_tile = ATT_TQ // ATT_TK
    m_run = lax.fori_loop(0, i * blocks_per_tile, step,
                          tuple(jnp.full((1, unit_w), -jnp.inf, F32) for _ in units))
    for d in range(blocks_per_tile):
        m_run = step(i * blocks_per_tile + d, m_run, diag_offset=d * ATT_TK)

    lam = (jnp.exp(jnp.sum(lq1_ref[...] * lk1_ref[...])) - jnp.exp(jnp.sum(lq2_ref[...] * lk2_ref[...]))
           + lambda_init)
    for g in range(ATT_G):
        ot = acc_ref[g, :DIFF_V_DIM, :] / acc_ref[g, DIFF_V_DIM:DIFF_V_DIM + 1, :]
        o = (ot[:, :ATT_TQ] - lam * ot[:, ATT_TQ:]).T
        o = _rms(o, gout_ref[...]) * (1.0 - lambda_init)
        o_ref[0, :, g * DIFF_V_DIM:(g + 1) * DIFF_V_DIM] = o.astype(BF16)


def _diff_attn(qt, k, vt, lq1, lk1, lq2, lk2, g_out, lambda_init):
    bsz, t, _ = k.shape
    gw = ATT_G * DIFF_V_DIM
    lam_spec = _resident((1, DIFF_HEAD_DIM))
    return pl.pallas_call(
        functools.partial(_diff_attn_body, lambda_init=lambda_init),
        grid=(bsz, DIFF_HEADS // ATT_G, t // ATT_TQ),
        in_specs=[pl.BlockSpec((1, gw, ATT_TQ), lambda b, h, i: (b, h, i)),
                  pl.BlockSpec((1, t, gw), lambda b, h, i: (b, 0, h)),
                  pl.BlockSpec((1, t // ATT_TK, gw, ATT_TK), lambda b, h, i: (b, 0, h, 0)),
                  lam_spec, lam_spec, lam_spec, lam_spec, _resident((1, DIFF_V_DIM))],
        out_specs=pl.BlockSpec((1, ATT_TQ, gw), lambda b, h, i: (b, i, h)),
        out_shape=jax.ShapeDtypeStruct(k.shape, BF16),
        scratch_shapes=[pltpu.VMEM((ATT_G, DIFF_V_DIM + BF16_ROWS, 2 * ATT_TQ), F32)],
        compiler_params=_params(("arbitrary", "arbitrary", "arbitrary")),
        name="diff_attn",
    )(qt, k, vt, lq1.reshape(1, -1), lk1.reshape(1, -1), lq2.reshape(1, -1), lk2.reshape(1, -1),
      g_out.reshape(1, -1))


def kernel(x, l0_norm_ffn1, l0_ffn1_w_in, l0_ffn1_w_down, l0_norm_mix, l0_gla_w_in, l0_gla_w_gate2, l0_gla_b_gate2, l0_gla_norm_out, l0_gla_w_out, l0_norm_ffn2, l0_ffn2_w_in, l0_ffn2_w_down, l1_norm_ffn1, l1_ffn1_w_in, l1_ffn1_w_down, l1_norm_mix, l1_diff_w_in, l1_diff_lambda_q1, l1_diff_lambda_k1, l1_diff_lambda_q2, l1_diff_lambda_k2, l1_diff_norm_out, l1_diff_w_out, l1_norm_ffn2, l1_ffn2_w_in, l1_ffn2_w_down, final_norm):
    bsz, t, d = x.shape
    m = bsz * t
    lambda_init = 0.8 - 0.6 * math.exp(-0.3 * 1)

    h = _ffn(x.reshape(m, d), l0_norm_ffn1, l0_ffn1_w_in, l0_ffn1_w_down, final_norm, False)
    h = _gla_layer(h, l0_norm_mix, l0_gla_w_in, l0_gla_w_gate2, l0_gla_b_gate2,
                   l0_gla_norm_out, l0_gla_w_out, t)
    h = _ffn(h, l0_norm_ffn2, l0_ffn2_w_in, l0_ffn2_w_down, final_norm, False)

    h = _ffn(h, l1_norm_ffn1, l1_ffn1_w_in, l1_ffn1_w_down, final_norm, False)
    qt, k, vt = _qkv_proj(h.reshape(bsz, t, d), l1_norm_mix, l1_diff_w_in)
    o = _diff_attn(qt, k, vt, l1_diff_lambda_q1, l1_diff_lambda_k1, l1_diff_lambda_q2, l1_diff_lambda_k2,
                   l1_diff_norm_out, lambda_init)
    h = _ffn(h, l1_norm_ffn2, l1_ffn2_w_in, l1_ffn2_w_down, final_norm, True,
             mixer_out=o.reshape(m, d), w_out=l1_diff_w_out)
    return h.reshape(bsz, t, d)
```

```python
import functools
import math

import jax
import jax.numpy as jnp
from jax import lax
from jax.experimental import pallas as pl
from jax.experimental.pallas import tpu as pltpu

F32 = jnp.float32
BF16 = jnp.bfloat16

D_MODEL = 1024
D_FF = 2816
NORM_EPS = 1e-6

GLA_HEADS = 4
GLA_DK = 512
GLA_DV = 1024
GLA_HEAD_K = 128
GLA_HEAD_V = 256
GLA_GATE_RANK = 16
GLA_TAU = 16.0
GLA_CHUNK = 64
GLA_MAIN = 2 * GLA_DK + 2 * GLA_DV

DIFF_HEADS = 8
DIFF_HEAD_DIM = 64
DIFF_V_DIM = 128

LOG2_E = math.log2(math.e)
LANES = 128
BF16_ROWS = 16
VMEM_LIMIT = 56 * 1024 * 1024

FFN_TM = 1024
FFN_TM_FUSED = 512
FFN_TF = 256
GLA_PIECE = 512
GLA_TM = 512
PROJ_TM = 1024
ATT_TQ = 512
ATT_TK = 512
ATT_G = 4

NT_DIMS = (((1,), (1,)), ((), ()))
TN_DIMS = (((0,), (0,)), ((), ()))


def _rms(x, g):
    ms = jnp.mean(x * x, axis=-1, keepdims=True)
    return (x * lax.rsqrt(ms + NORM_EPS)) * g


def _silu(x):
    return x * (1.0 / (1.0 + jnp.exp(-x)))


def _resident(shape):
    nd = len(shape)
    return pl.BlockSpec(shape, lambda *_: (0,) * nd, pipeline_mode=pl.Buffered(1))


def _params(sem):
    return pltpu.CompilerParams(dimension_semantics=sem, vmem_limit_bytes=VMEM_LIMIT)


def _ffn_body(*refs, final_norm, fused_proj):
    if fused_proj:
        x_ref, o_ref, wout_ref, g_ref, win_ref, wdown_ref, gfin_ref, y_ref = refs
        x = x_ref[...] + jnp.dot(o_ref[...], wout_ref[...].astype(BF16), preferred_element_type=F32)
    else:
        x_ref, g_ref, win_ref, wdown_ref, gfin_ref, y_ref = refs
        x = x_ref[...]
    xn = _rms(x, g_ref[...]).astype(BF16)
    acc = None
    for c in range(D_FF // FFN_TF):
        lo = c * FFN_TF
        w_gate = win_ref[:, lo:lo + FFN_TF].astype(BF16)
        w_up = win_ref[:, D_FF + lo:D_FF + lo + FFN_TF].astype(BF16)
        gate = jnp.dot(xn, w_gate, preferred_element_type=F32)
        up = jnp.dot(xn, w_up, preferred_element_type=F32)
        h = (_silu(gate) * up).astype(BF16)
        part = jnp.dot(h, wdown_ref[lo:lo + FFN_TF, :].astype(BF16), preferred_element_type=F32)
        acc = part if acc is None else acc + part
    y = x + 0.5 * acc
    if final_norm:
        y = _rms(y, gfin_ref[...])
    y_ref[...] = y


def _ffn(x2d, g, w_in, w_down, g_final, final_norm, mixer_out=None, w_out=None):
    m = x2d.shape[0]
    fused = mixer_out is not None
    tm = FFN_TM_FUSED if fused else FFN_TM
    row = pl.BlockSpec((tm, D_MODEL), lambda i: (i, 0))
    args, specs = [x2d], [row]
    if fused:
        args += [mixer_out, w_out]
        specs += [row, _resident((D_MODEL, D_MODEL))]
    args += [g.reshape(1, -1), w_in, w_down, g_final.reshape(1, -1)]
    specs += [_resident((1, D_MODEL)), _resident((D_MODEL, 2 * D_FF)), _resident((D_FF, D_MODEL)),
              _resident((1, D_MODEL))]
    return pl.pallas_call(
        functools.partial(_ffn_body, final_norm=final_norm, fused_proj=fused),
        grid=(m // tm,),
        in_specs=specs,
        out_specs=row,
        out_shape=jax.ShapeDtypeStruct((m, D_MODEL), F32),
        compiler_params=_params(("arbitrary",)),
        name="ffn",
    )(*args)


def _gla_body(xn_ref, xp_ref, gmix_ref, win_ref, wg1_ref, wg2_ref, bg2_ref, gout_ref, wout_ref,
              o_ref, proj_ref, la_ref, state_ref, og_ref, *, tiles_per_seq):
    g = pl.program_id(0)
    nxt = g % 2
    prv = 1 - nxt

    @pl.when(g == 0)
    def _():
        proj_ref[...] = jnp.zeros_like(proj_ref)
        la_ref[...] = jnp.zeros_like(la_ref)

    @pl.when((g == 0) | ((g - 1) % tiles_per_seq == 0))
    def _():
        state_ref[...] = jnp.zeros_like(state_ref)

    hn = _rms(xn_ref[...], gmix_ref[...]).astype(BF16)

    rows = lax.broadcasted_iota(jnp.int32, (GLA_CHUNK, GLA_CHUNK), 0)
    cols = lax.broadcasted_iota(jnp.int32, (GLA_CHUNK, GLA_CHUNK), 1)
    causal = rows >= cols
    tril = causal.astype(BF16)
    scale = GLA_HEAD_K ** -0.5
    n_chunks = GLA_TM // GLA_CHUNK

    def project(piece):
        if piece < GLA_MAIN // GLA_PIECE:
            c0 = piece * GLA_PIECE
            proj_ref[nxt, :, c0:c0 + GLA_PIECE] = jnp.dot(hn, win_ref[:, c0:c0 + GLA_PIECE],
                                                          preferred_element_type=F32)
        else:
            g_lr = jnp.dot(hn, wg1_ref[...], preferred_element_type=F32)
            z = jnp.dot(g_lr.astype(BF16), wg2_ref[...], preferred_element_type=F32) + bg2_ref[...]
            la_ref[nxt] = (jnp.minimum(z, 0.0) - jnp.log(1.0 + jnp.exp(-jnp.abs(z)))) * (1.0 / GLA_TAU)

    def rows_of(n):
        return slice(n * GLA_CHUNK, (n + 1) * GLA_CHUNK)

    def ks(h):
        return slice(h * GLA_HEAD_K, (h + 1) * GLA_HEAD_K)

    def vs(h):
        return slice(h * GLA_HEAD_V, (h + 1) * GLA_HEAD_V)

    chunk_heads = [(n, h) for n in range(n_chunks) for h in range(GLA_HEADS)]
    project(0)
    cum = []
    for n in range(n_chunks):
        la = la_ref[prv, rows_of(n), :]
        la_hi = la.astype(BF16)
        la_r = la - la_hi.astype(F32)
        la_mid = la_r.astype(BF16)
        la_lo = (la_r - la_mid.astype(F32)).astype(BF16)
        cum.append(jnp.dot(tril, la_hi, preferred_element_type=F32)
                   + jnp.dot(tril, la_mid, preferred_element_type=F32)
                   + jnp.dot(tril, la_lo, preferred_element_type=F32))

    project(1)
    q_dec, k_inv, k_tail, decay, v = [], [], [], [], []
    for n in range(n_chunks):
        b = cum[n]
        b_last = b[GLA_CHUNK - 1:GLA_CHUNK]
        q = proj_ref[prv, rows_of(n), 0:GLA_DK]
        k = proj_ref[prv, rows_of(n), GLA_DK:2 * GLA_DK]
        q_dec.append(((q * scale) * jnp.exp(b)).astype(BF16))
        k_inv.append((k * jnp.exp(-b)).astype(BF16))
        k_tail.append((k * jnp.exp(b_last - b)).astype(BF16))
        decay.append(jnp.exp(b_last))
        v.append(proj_ref[prv, rows_of(n), 2 * GLA_DK:2 * GLA_DK + GLA_DV].astype(BF16))

    project(2)
    att = {}
    for n, h in chunk_heads:
        a = lax.dot_general(q_dec[n][:, ks(h)], k_inv[n][:, ks(h)], NT_DIMS, preferred_element_type=F32)
        att[n, h] = jnp.where(causal, a, 0.0).astype(BF16)

    project(3)
    upd = {}
    for n, h in chunk_heads:
        upd[n, h] = lax.dot_general(v[n][:, vs(h)], k_tail[n][:, ks(h)], TN_DIMS,
                                    preferred_element_type=F32)

    project(4)
    start = {}
    for h in range(GLA_HEADS):
        s_t = state_ref[h]
        for n in range(n_chunks):
            start[n, h] = s_t.astype(BF16)
            s_t = s_t * decay[n][:, ks(h)] + upd[n, h]
        state_ref[h] = s_t

    project(5)
    outs = {}
    for n, h in chunk_heads:
        o = jnp.dot(att[n, h], v[n][:, vs(h)], preferred_element_type=F32)
        outs[n, h] = o + lax.dot_general(q_dec[n][:, ks(h)], start[n, h], NT_DIMS, preferred_element_type=F32)

    project(6)
    for n, h in chunk_heads:
        r0 = 2 * GLA_DK + GLA_DV + h * GLA_HEAD_V
        r = proj_ref[prv, rows_of(n), r0:r0 + GLA_HEAD_V]
        o = _rms(outs[n, h], gout_ref[...])
        og_ref[rows_of(n), vs(h)] = (o * _silu(r)).astype(BF16)

    o_ref[...] = xp_ref[...] + jnp.dot(og_ref[...], wout_ref[...], preferred_element_type=F32)


def _gla_layer(x2d, g_mix, w_in, w_gate2, b_gate2, g_out, w_out, seq_len):
    m = x2d.shape[0]
    n_tiles = m // GLA_TM
    w_main = w_in[:, :GLA_MAIN].astype(BF16)
    w_g1 = jnp.pad(w_in[:, GLA_MAIN:], ((0, 0), (0, LANES - GLA_GATE_RANK))).astype(BF16)
    w_g2 = jnp.pad(w_gate2, ((0, LANES - GLA_GATE_RANK), (0, 0))).astype(BF16)
    nxt_spec = pl.BlockSpec((GLA_TM, D_MODEL), lambda g: (jnp.minimum(g, n_tiles - 1), 0))
    prv_spec = pl.BlockSpec((GLA_TM, D_MODEL), lambda g: (jnp.maximum(g - 1, 0), 0))
    return pl.pallas_call(
        functools.partial(_gla_body, tiles_per_seq=seq_len // GLA_TM),
        grid=(n_tiles + 1,),
        in_specs=[nxt_spec, prv_spec, _resident((1, D_MODEL)), _resident((D_MODEL, GLA_MAIN)),
                  _resident((D_MODEL, LANES)), _resident((LANES, GLA_DK)), _resident((1, GLA_DK)),
                  _resident((1, GLA_HEAD_V)), _resident((GLA_DV, D_MODEL))],
        out_specs=prv_spec,
        out_shape=jax.ShapeDtypeStruct((m, D_MODEL), F32),
        scratch_shapes=[pltpu.VMEM((2, GLA_TM, GLA_MAIN), F32),
                        pltpu.VMEM((2, GLA_TM, GLA_DK), F32),
                        pltpu.VMEM((GLA_HEADS, GLA_HEAD_V, GLA_HEAD_K), F32),
                        pltpu.VMEM((GLA_TM, GLA_DV), BF16)],
        compiler_params=_params(("arbitrary",)),
        name="gla_layer",
    )(x2d, x2d, g_mix.reshape(1, -1), w_main, w_g1, w_g2, b_gate2.reshape(1, -1),
      g_out.reshape(1, -1), w_out.astype(BF16))


def _qkv_body(x_ref, g_ref, wqt_ref, wk_ref, wvt_ref, qt_ref, k_ref, vt_ref):
    hn = _rms(x_ref[0], g_ref[...]).astype(BF16)
    qt = lax.dot_general(wqt_ref[...], hn, NT_DIMS, preferred_element_type=F32)
    qt_ref[0] = (qt * (DIFF_HEAD_DIM ** -0.5 * LOG2_E)).astype(BF16)
    k_ref[0] = jnp.dot(hn, wk_ref[...], preferred_element_type=F32).astype(BF16)
    vt = lax.dot_general(wvt_ref[...], hn, NT_DIMS, preferred_element_type=F32)
    for c in range(PROJ_TM // ATT_TK):
        vt_ref[0, c] = vt[:, c * ATT_TK:(c + 1) * ATT_TK].astype(BF16)


def _qkv_proj(x, g, w_in):
    bsz, t, _ = x.shape
    w_q, w_k, w_v = jnp.split(w_in.astype(BF16), 3, axis=1)
    return pl.pallas_call(
        _qkv_body,
        grid=(bsz, t // PROJ_TM),
        in_specs=[pl.BlockSpec((1, PROJ_TM, D_MODEL), lambda b, i: (b, i, 0)),
                  _resident((1, D_MODEL)), _resident((D_MODEL, D_MODEL)),
                  _resident((D_MODEL, D_MODEL)), _resident((D_MODEL, D_MODEL))],
        out_specs=[pl.BlockSpec((1, D_MODEL, PROJ_TM), lambda b, i: (b, 0, i)),
                   pl.BlockSpec((1, PROJ_TM, D_MODEL), lambda b, i: (b, i, 0)),
                   pl.BlockSpec((1, PROJ_TM // ATT_TK, D_MODEL, ATT_TK), lambda b, i: (b, i, 0, 0))],
        out_shape=[jax.ShapeDtypeStruct((bsz, D_MODEL, t), BF16),
                   jax.ShapeDtypeStruct((bsz, t, D_MODEL), BF16),
                   jax.ShapeDtypeStruct((bsz, t // ATT_TK, D_MODEL, ATT_TK), BF16)],
        compiler_params=_params(("arbitrary", "arbitrary")),
        name="diff_qkv",
    )(x, g.reshape(1, -1), w_q.T, w_k, w_v.T)


def _diff_attn_body(qt_ref, k_ref, vt_ref, lq1_ref, lk1_ref, lq2_ref, lk2_ref, gout_ref, o_ref, acc_ref,
                    *, lambda_init):
    i = pl.program_id(2)
    m2 = 2 * ATT_TQ
    feat = lax.broadcasted_iota(jnp.int32, (DIFF_V_DIM, ATT_TQ), 0)
    q2t = []
    for g in range(ATT_G):
        qt = qt_ref[0, g * DIFF_V_DIM:(g + 1) * DIFF_V_DIM, :].astype(F32)
        q2t.append(jnp.concatenate([jnp.where(feat < DIFF_HEAD_DIM, qt, 0.0),
                                    jnp.where(feat >= DIFF_HEAD_DIM, qt, 0.0)], axis=1).astype(BF16))

    heads = [slice(g * DIFF_V_DIM, (g + 1) * DIFF_V_DIM) for g in range(ATT_G)]

    def step(j, m_run, diag_offset=None):
        scores, probs = {}, {}

        def score(g):
            k = k_ref[0, pl.ds(j * ATT_TK, ATT_TK), heads[g]]
            s = jnp.dot(k, q2t[g], preferred_element_type=F32)
            if diag_offset is not None:
                kpos = lax.broadcasted_iota(jnp.int32, s.shape, 0) + diag_offset
                qpos = lax.broadcasted_iota(jnp.int32, s.shape, 1) % ATT_TQ
                s = jnp.where(qpos >= kpos, s, -jnp.inf)
            scores[g] = s

        def softmax(g):
            m_new = jnp.maximum(m_run[g], jnp.max(scores[g], axis=0, keepdims=True))
            probs[g] = (m_new, jnp.exp2(m_run[g] - m_new), jnp.exp2(scores[g] - m_new).astype(BF16))

        def value(g):
            _, alpha, p = probs[g]
            vt1 = jnp.concatenate([vt_ref[0, j, heads[g], :], ones_rows], axis=0)
            acc_ref[g] = alpha * acc_ref[g] + jnp.dot(vt1, p, preferred_element_type=F32)

        for t in range(ATT_G + 1):
            if t < ATT_G:
                score(t)
            if t >= 1:
                softmax(t - 1)
                value(t - 1)
        return tuple(probs[g][0] for g in range(ATT_G))

    ones_rows = jnp.ones((BF16_ROWS, ATT_TK), BF16)
    acc_ref[...] = jnp.zeros_like(acc_ref)
    blocks_per_tile = ATT_TQ // ATT_TK
    m_run = lax.fori_loop(0, i * blocks_per_tile, step,
                          tuple(jnp.full((1, m2), -jnp.inf, F32) for _ in range(ATT_G)))
    for d in range(blocks_per_tile):
        m_run = step(i * blocks_per_tile + d, m_run, diag_offset=d * ATT_TK)

    lam = (jnp.exp(jnp.sum(lq1_ref[...] * lk1_ref[...])) - jnp.exp(jnp.sum(lq2_ref[...] * lk2_ref[...]))
           + lambda_init)
    for g in range(ATT_G):
        ot = acc_ref[g, :DIFF_V_DIM, :] / acc_ref[g, DIFF_V_DIM:DIFF_V_DIM + 1, :]
        o = (ot[:, :ATT_TQ] - lam * ot[:, ATT_TQ:]).T
        o = _rms(o, gout_ref[...]) * (1.0 - lambda_init)
        o_ref[0, :, g * DIFF_V_DIM:(g + 1) * DIFF_V_DIM] = o.astype(BF16)


def _diff_attn(qt, k, vt, lq1, lk1, lq2, lk2, g_out, lambda_init):
    bsz, t, _ = k.shape
    gw = ATT_G * DIFF_V_DIM
    lam_spec = _resident((1, DIFF_HEAD_DIM))
    return pl.pallas_call(
        functools.partial(_diff_attn_body, lambda_init=lambda_init),
        grid=(bsz, DIFF_HEADS // ATT_G, t // ATT_TQ),
        in_specs=[pl.BlockSpec((1, gw, ATT_TQ), lambda b, h, i: (b, h, i)),
                  pl.BlockSpec((1, t, gw), lambda b, h, i: (b, 0, h)),
                  pl.BlockSpec((1, t // ATT_TK, gw, ATT_TK), lambda b, h, i: (b, 0, h, 0)),
                  lam_spec, lam_spec, lam_spec, lam_spec, _resident((1, DIFF_V_DIM))],
        out_specs=pl.BlockSpec((1, ATT_TQ, gw), lambda b, h, i: (b, i, h)),
        out_shape=jax.ShapeDtypeStruct(k.shape, BF16),
        scratch_shapes=[pltpu.VMEM((ATT_G, DIFF_V_DIM + BF16_ROWS, 2 * ATT_TQ), F32)],
        compiler_params=_params(("arbitrary", "arbitrary", "arbitrary")),
        name="diff_attn",
    )(qt, k, vt, lq1.reshape(1, -1), lk1.reshape(1, -1), lq2.reshape(1, -1), lk2.reshape(1, -1),
      g_out.reshape(1, -1))


def kernel(x, l0_norm_ffn1, l0_ffn1_w_in, l0_ffn1_w_down, l0_norm_mix, l0_gla_w_in, l0_gla_w_gate2, l0_gla_b_gate2, l0_gla_norm_out, l0_gla_w_out, l0_norm_ffn2, l0_ffn2_w_in, l0_ffn2_w_down, l1_norm_ffn1, l1_ffn1_w_in, l1_ffn1_w_down, l1_norm_mix, l1_diff_w_in, l1_diff_lambda_q1, l1_diff_lambda_k1, l1_diff_lambda_q2, l1_diff_lambda_k2, l1_diff_norm_out, l1_diff_w_out, l1_norm_ffn2, l1_ffn2_w_in, l1_ffn2_w_down, final_norm):
    bsz, t, d = x.shape
    m = bsz * t
    lambda_init = 0.8 - 0.6 * math.exp(-0.3 * 1)

    h = _ffn(x.reshape(m, d), l0_norm_ffn1, l0_ffn1_w_in, l0_ffn1_w_down, final_norm, False)
    h = _gla_layer(h, l0_norm_mix, l0_gla_w_in, l0_gla_w_gate2, l0_gla_b_gate2,
                   l0_gla_norm_out, l0_gla_w_out, t)
    h = _ffn(h, l0_norm_ffn2, l0_ffn2_w_in, l0_ffn2_w_down, final_norm, False)

    h = _ffn(h, l1_norm_ffn1, l1_ffn1_w_in, l1_ffn1_w_down, final_norm, False)
    qt, k, vt = _qkv_proj(h.reshape(bsz, t, d), l1_norm_mix, l1_diff_w_in)
    o = _diff_attn(qt, k, vt, l1_diff_lambda_q1, l1_diff_lambda_k1, l1_diff_lambda_q2, l1_diff_lambda_k2,
                   l1_diff_norm_out, lambda_init)
    h = _ffn(h, l1_norm_ffn2, l1_ffn2_w_in, l1_ffn2_w_down, final_norm, True,
             mixer_out=o.reshape(m, d), w_out=l1_diff_w_out)
    return h.reshape(bsz, t, d)
```

```python
import functools
import math

import jax
import jax.numpy as jnp
from jax import lax
from jax.experimental import pallas as pl
from jax.experimental.pallas import tpu as pltpu

F32 = jnp.float32
BF16 = jnp.bfloat16

D_MODEL = 1024
D_FF = 2816
NORM_EPS = 1e-6

GLA_HEADS = 4
GLA_DK = 512
GLA_DV = 1024
GLA_HEAD_K = 128
GLA_HEAD_V = 256
GLA_GATE_RANK = 16
GLA_TAU = 16.0
GLA_CHUNK = 64
GLA_MAIN = 2 * GLA_DK + 2 * GLA_DV

DIFF_HEADS = 8
DIFF_HEAD_DIM = 64
DIFF_V_DIM = 128

LOG2_E = math.log2(math.e)
LANES = 128
BF16_ROWS = 16
VMEM_LIMIT = 56 * 1024 * 1024

FFN_TM = 1024
FFN_TM_FUSED = 512
FFN_TF = 256
GLA_PIECE = 512
GLA_TM = 512
PROJ_TM = 1024
ATT_TQ = 512
ATT_TK = 512
ATT_G = 4

NT_DIMS = (((1,), (1,)), ((), ()))
TN_DIMS = (((0,), (0,)), ((), ()))


def _rms(x, g):
    ms = jnp.mean(x * x, axis=-1, keepdims=True)
    return (x * lax.rsqrt(ms + NORM_EPS)) * g


def _silu(x):
    return x * (1.0 / (1.0 + jnp.exp(-x)))


def _resident(shape):
    nd = len(shape)
    return pl.BlockSpec(shape, lambda *_: (0,) * nd, pipeline_mode=pl.Buffered(1))


def _params(sem):
    return pltpu.CompilerParams(dimension_semantics=sem, vmem_limit_bytes=VMEM_LIMIT)


def _ffn_body(*refs, final_norm, fused_proj):
    if fused_proj:
        x_ref, o_ref, wout_ref, g_ref, win_ref, wdown_ref, gfin_ref, y_ref = refs
        x = x_ref[...] + jnp.dot(o_ref[...], wout_ref[...].astype(BF16), preferred_element_type=F32)
    else:
        x_ref, g_ref, win_ref, wdown_ref, gfin_ref, y_ref = refs
        x = x_ref[...]
    xn = _rms(x, g_ref[...]).astype(BF16)
    acc = None
    for c in range(D_FF // FFN_TF):
        lo = c * FFN_TF
        w_gate = win_ref[:, lo:lo + FFN_TF].astype(BF16)
        w_up = win_ref[:, D_FF + lo:D_FF + lo + FFN_TF].astype(BF16)
        gate = jnp.dot(xn, w_gate, preferred_element_type=F32)
        up = jnp.dot(xn, w_up, preferred_element_type=F32)
        h = (_silu(gate) * up).astype(BF16)
        part = jnp.dot(h, wdown_ref[lo:lo + FFN_TF, :].astype(BF16), preferred_element_type=F32)
        acc = part if acc is None else acc + part
    y = x + 0.5 * acc
    if final_norm:
        y = _rms(y, gfin_ref[...])
    y_ref[...] = y


def _ffn(x2d, g, w_in, w_down, g_final, final_norm, mixer_out=None, w_out=None):
    m = x2d.shape[0]
    fused = mixer_out is not None
    tm = FFN_TM_FUSED if fused else FFN_TM
    row = pl.BlockSpec((tm, D_MODEL), lambda i: (i, 0))
    args, specs = [x2d], [row]
    if fused:
        args += [mixer_out, w_out]
        specs += [row, _resident((D_MODEL, D_MODEL))]
    args += [g.reshape(1, -1), w_in, w_down, g_final.reshape(1, -1)]
    specs += [_resident((1, D_MODEL)), _resident((D_MODEL, 2 * D_FF)), _resident((D_FF, D_MODEL)),
              _resident((1, D_MODEL))]
    return pl.pallas_call(
        functools.partial(_ffn_body, final_norm=final_norm, fused_proj=fused),
        grid=(m // tm,),
        in_specs=specs,
        out_specs=row,
        out_shape=jax.ShapeDtypeStruct((m, D_MODEL), F32),
        compiler_params=_params(("arbitrary",)),
        name="ffn",
    )(*args)


def _gla_body(xn_ref, xp_ref, gmix_ref, win_ref, wg1_ref, wg2_ref, bg2_ref, gout_ref, wout_ref,
              o_ref, proj_ref, la_ref, state_ref, og_ref, *, tiles_per_seq):
    g = pl.program_id(0)
    nxt = g % 2
    prv = 1 - nxt

    @pl.when(g == 0)
    def _():
        proj_ref[...] = jnp.zeros_like(proj_ref)
        la_ref[...] = jnp.zeros_like(la_ref)

    @pl.when((g == 0) | ((g - 1) % tiles_per_seq == 0))
    def _():
        state_ref[...] = jnp.zeros_like(state_ref)

    hn = _rms(xn_ref[...], gmix_ref[...]).astype(BF16)

    rows = lax.broadcasted_iota(jnp.int32, (GLA_CHUNK, GLA_CHUNK), 0)
    cols = lax.broadcasted_iota(jnp.int32, (GLA_CHUNK, GLA_CHUNK), 1)
    causal = rows >= cols
    tril = causal.astype(BF16)
    scale = GLA_HEAD_K ** -0.5
    n_chunks = GLA_TM // GLA_CHUNK

    def project(piece):
        if piece < GLA_MAIN // GLA_PIECE:
            c0 = piece * GLA_PIECE
            proj_ref[nxt, :, c0:c0 + GLA_PIECE] = jnp.dot(hn, win_ref[:, c0:c0 + GLA_PIECE],
                                                          preferred_element_type=F32)
        else:
            g_lr = jnp.dot(hn, wg1_ref[...], preferred_element_type=F32)
            z = jnp.dot(g_lr.astype(BF16), wg2_ref[...], preferred_element_type=F32) + bg2_ref[...]
            la_ref[nxt] = (jnp.minimum(z, 0.0) - jnp.log(1.0 + jnp.exp(-jnp.abs(z)))) * (1.0 / GLA_TAU)

    def rows_of(n):
        return slice(n * GLA_CHUNK, (n + 1) * GLA_CHUNK)

    def ks(h):
        return slice(h * GLA_HEAD_K, (h + 1) * GLA_HEAD_K)

    def vs(h):
        return slice(h * GLA_HEAD_V, (h + 1) * GLA_HEAD_V)

    chunk_heads = [(n, h) for n in range(n_chunks) for h in range(GLA_HEADS)]
    project(0)
    cum = []
    for n in range(n_chunks):
        la = la_ref[prv, rows_of(n), :]
        la_hi = la.astype(BF16)
        la_r = la - la_hi.astype(F32)
        la_mid = la_r.astype(BF16)
        la_lo = (la_r - la_mid.astype(F32)).astype(BF16)
        cum.append(jnp.dot(tril, la_hi, preferred_element_type=F32)
                   + jnp.dot(tril, la_mid, preferred_element_type=F32)
                   + jnp.dot(tril, la_lo, preferred_element_type=F32))

    project(1)
    q_dec, k_inv, k_tail, decay, v = [], [], [], [], []
    for n in range(n_chunks):
        b = cum[n]
        b_last = b[GLA_CHUNK - 1:GLA_CHUNK]
        q = proj_ref[prv, rows_of(n), 0:GLA_DK]
        k = proj_ref[prv, rows_of(n), GLA_DK:2 * GLA_DK]
        q_dec.append(((q * scale) * jnp.exp(b)).astype(BF16))
        k_inv.append((k * jnp.exp(-b)).astype(BF16))
        k_tail.append((k * jnp.exp(b_last - b)).astype(BF16))
        decay.append(jnp.exp(b_last))
        v.append(proj_ref[prv, rows_of(n), 2 * GLA_DK:2 * GLA_DK + GLA_DV].astype(BF16))

    project(2)
    att = {}
    for n, h in chunk_heads:
        a = lax.dot_general(q_dec[n][:, ks(h)], k_inv[n][:, ks(h)], NT_DIMS, preferred_element_type=F32)
        att[n, h] = jnp.where(causal, a, 0.0).astype(BF16)

    project(3)
    upd = {}
    for n, h in chunk_heads:
        upd[n, h] = lax.dot_general(v[n][:, vs(h)], k_tail[n][:, ks(h)], TN_DIMS,
                                    preferred_element_type=F32)

    project(4)
    start = {}
    for h in range(GLA_HEADS):
        s_t = state_ref[h]
        for n in range(n_chunks):
            start[n, h] = s_t.astype(BF16)
            s_t = s_t * decay[n][:, ks(h)] + upd[n, h]
        state_ref[h] = s_t

    project(5)
    outs = {}
    for n, h in chunk_heads:
        o = jnp.dot(att[n, h], v[n][:, vs(h)], preferred_element_type=F32)
        outs[n, h] = o + lax.dot_general(q_dec[n][:, ks(h)], start[n, h], NT_DIMS, preferred_element_type=F32)

    project(6)
    for n, h in chunk_heads:
        r0 = 2 * GLA_DK + GLA_DV + h * GLA_HEAD_V
        r = proj_ref[prv, rows_of(n), r0:r0 + GLA_HEAD_V]
        o = _rms(outs[n, h], gout_ref[...])
        og_ref[rows_of(n), vs(h)] = (o * _silu(r)).astype(BF16)

    o_ref[...] = xp_ref[...] + jnp.dot(og_ref[...], wout_ref[...], preferred_element_type=F32)


def _gla_layer(x2d, g_mix, w_in, w_gate2, b_gate2, g_out, w_out, seq_len):
    m = x2d.shape[0]
    n_tiles = m // GLA_TM
    w_main = w_in[:, :GLA_MAIN].astype(BF16)
    w_g1 = jnp.pad(w_in[:, GLA_MAIN:], ((0, 0), (0, LANES - GLA_GATE_RANK))).astype(BF16)
    w_g2 = jnp.pad(w_gate2, ((0, LANES - GLA_GATE_RANK), (0, 0))).astype(BF16)
    nxt_spec = pl.BlockSpec((GLA_TM, D_MODEL), lambda g: (jnp.minimum(g, n_tiles - 1), 0))
    prv_spec = pl.BlockSpec((GLA_TM, D_MODEL), lambda g: (jnp.maximum(g - 1, 0), 0))
    return pl.pallas_call(
        functools.partial(_gla_body, tiles_per_seq=seq_len // GLA_TM),
        grid=(n_tiles + 1,),
        in_specs=[nxt_spec, prv_spec, _resident((1, D_MODEL)), _resident((D_MODEL, GLA_MAIN)),
                  _resident((D_MODEL, LANES)), _resident((LANES, GLA_DK)), _resident((1, GLA_DK)),
                  _resident((1, GLA_HEAD_V)), _resident((GLA_DV, D_MODEL))],
        out_specs=prv_spec,
        out_shape=jax.ShapeDtypeStruct((m, D_MODEL), F32),
        scratch_shapes=[pltpu.VMEM((2, GLA_TM, GLA_MAIN), F32),
                        pltpu.VMEM((2, GLA_TM, GLA_DK), F32),
                        pltpu.VMEM((GLA_HEADS, GLA_HEAD_V, GLA_HEAD_K), F32),
                        pltpu.VMEM((GLA_TM, GLA_DV), BF16)],
        compiler_params=_params(("arbitrary",)),
        name="gla_layer",
    )(x2d, x2d, g_mix.reshape(1, -1), w_main, w_g1, w_g2, b_gate2.reshape(1, -1),
      g_out.reshape(1, -1), w_out.astype(BF16))


def _qkv_body(x_ref, g_ref, wqt_ref, wk_ref, wvt_ref, qt_ref, k_ref, vt_ref):
    hn = _rms(x_ref[0], g_ref[...]).astype(BF16)
    qt = lax.dot_general(wqt_ref[...], hn, NT_DIMS, preferred_element_type=F32)
    qt_ref[0] = (qt * (DIFF_HEAD_DIM ** -0.5 * LOG2_E)).astype(BF16)
    k_ref[0] = jnp.dot(hn, wk_ref[...], preferred_element_type=F32).astype(BF16)
    vt = lax.dot_general(wvt_ref[...], hn, NT_DIMS, preferred_element_type=F32)
    for c in range(PROJ_TM // ATT_TK):
        vt_ref[0, c] = vt[:, c * ATT_TK:(c + 1) * ATT_TK].astype(BF16)


def _qkv_proj(x, g, w_in):
    bsz, t, _ = x.shape
    w_q, w_k, w_v = jnp.split(w_in.astype(BF16), 3, axis=1)
    return pl.pallas_call(
        _qkv_body,
        grid=(bsz, t // PROJ_TM),
        in_specs=[pl.BlockSpec((1, PROJ_TM, D_MODEL), lambda b, i: (b, i, 0)),
                  _resident((1, D_MODEL)), _resident((D_MODEL, D_MODEL)),
                  _resident((D_MODEL, D_MODEL)), _resident((D_MODEL, D_MODEL))],
        out_specs=[pl.BlockSpec((1, D_MODEL, PROJ_TM), lambda b, i: (b, 0, i)),
                   pl.BlockSpec((1, PROJ_TM, D_MODEL), lambda b, i: (b, i, 0)),
                   pl.BlockSpec((1, PROJ_TM // ATT_TK, D_MODEL, ATT_TK), lambda b, i: (b, i, 0, 0))],
        out_shape=[jax.ShapeDtypeStruct((bsz, D_MODEL, t), BF16),
                   jax.ShapeDtypeStruct((bsz, t, D_MODEL), BF16),
                   jax.ShapeDtypeStruct((bsz, t // ATT_TK, D_MODEL, ATT_TK), BF16)],
        compiler_params=_params(("arbitrary", "arbitrary")),
        name="diff_qkv",
    )(x, g.reshape(1, -1), w_q.T, w_k, w_v.T)


def _diff_attn_body(qt_ref, k_ref, vt_ref, lq1_ref, lk1_ref, lq2_ref, lk2_ref, gout_ref, o_ref, acc_ref,
                    *, lambda_init):
    i = pl.program_id(2)
    m2 = 2 * ATT_TQ
    feat = lax.broadcasted_iota(jnp.int32, (DIFF_V_DIM, ATT_TQ), 0)
    q2t = []
    for g in range(ATT_G):
        qt = qt_ref[0, g * DIFF_V_DIM:(g + 1) * DIFF_V_DIM, :].astype(F32)
        q2t.append(jnp.concatenate([jnp.where(feat < DIFF_HEAD_DIM, qt, 0.0),
                                    jnp.where(feat >= DIFF_HEAD_DIM, qt, 0.0)], axis=1).astype(BF16))

    heads = [slice(g * DIFF_V_DIM, (g + 1) * DIFF_V_DIM) for g in range(ATT_G)]

    def step(j, m_run, diag_offset=None):
        scores, probs = {}, {}

        def score(g):
            k = k_ref[0, pl.ds(j * ATT_TK, ATT_TK), heads[g]]
            s = jnp.dot(k, q2t[g], preferred_element_type=F32)
            if diag_offset is not None:
                kpos = lax.broadcasted_iota(jnp.int32, s.shape, 0) + diag_offset
                qpos = lax.broadcasted_iota(jnp.int32, s.shape, 1) % ATT_TQ
                s = jnp.where(qpos >= kpos, s, -jnp.inf)
            scores[g] = s

        def softmax(g):
            m_new = jnp.maximum(m_run[g], jnp.max(scores[g], axis=0, keepdims=True))
            probs[g] = (m_new, jnp.exp2(m_run[g] - m_new), jnp.exp2(scores[g] - m_new).astype(BF16))

        def value(g):
            _, alpha, p = probs[g]
            vt1 = jnp.concatenate([vt_ref[0, j, heads[g], :], ones_rows], axis=0)
            acc_ref[g] = alpha * acc_ref[g] + jnp.dot(vt1, p, preferred_element_type=F32)

        for t in range(ATT_G + 1):
            if t < ATT_G:
                score(t)
            if t >= 1:
                softmax(t - 1)
                value(t - 1)
        return tuple(probs[g][0] for g in range(ATT_G))

    half = ATT_TK // 2

    def later_half(x):
        return jnp.concatenate([x[:, half:ATT_TQ], x[:, ATT_TQ + half:]], axis=1)

    def diagonal_step(m_run):
        scores, probs = {}, {}

        def score(g):
            k_lo = k_ref[0, pl.ds(i * ATT_TK, half), heads[g]]
            k_hi = k_ref[0, pl.ds(i * ATT_TK + half, half), heads[g]]
            s_lo = jnp.dot(k_lo, q2t[g], preferred_element_type=F32)
            s_hi = jnp.dot(k_hi, later_half(q2t[g]), preferred_element_type=F32)
            kpos = lax.broadcasted_iota(jnp.int32, s_lo.shape, 0)
            qpos = lax.broadcasted_iota(jnp.int32, s_lo.shape, 1) % ATT_TQ
            s_lo = jnp.where(qpos >= kpos, s_lo, -jnp.inf)
            kpos = lax.broadcasted_iota(jnp.int32, s_hi.shape, 0)
            qpos = lax.broadcasted_iota(jnp.int32, s_hi.shape, 1) % half
            s_hi = jnp.where(qpos >= kpos, s_hi, -jnp.inf)
            scores[g] = (s_lo, s_hi)

        def softmax(g):
            s_lo, s_hi = scores[g]
            m_lo = jnp.max(s_lo, axis=0, keepdims=True)
            m_hi = jnp.max(s_hi, axis=0, keepdims=True)
            m_blk = jnp.concatenate([m_lo[:, :half], jnp.maximum(m_lo[:, half:ATT_TQ], m_hi[:, :half]),
                                     m_lo[:, ATT_TQ:ATT_TQ + half],
                                     jnp.maximum(m_lo[:, ATT_TQ + half:], m_hi[:, half:])], axis=1)
            m_new = jnp.maximum(m_run[g], m_blk)
            probs[g] = (jnp.exp2(m_run[g] - m_new), jnp.exp2(s_lo - m_new).astype(BF16),
                        jnp.exp2(s_hi - later_half(m_new)).astype(BF16))

        def value(g):
            alpha, p_lo, p_hi = probs[g]
            vt1 = jnp.concatenate([vt_ref[0, i, heads[g], :], ones_rows], axis=0)
            lo = alpha * acc_ref[g] + jnp.dot(vt1[:, :half], p_lo, preferred_element_type=F32)
            hi = jnp.dot(vt1[:, half:], p_hi, preferred_element_type=F32)
            acc_ref[g] = jnp.concatenate([lo[:, :half], lo[:, half:ATT_TQ] + hi[:, :half],
                                          lo[:, ATT_TQ:ATT_TQ + half], lo[:, ATT_TQ + half:] + hi[:, half:]],
                                         axis=1)

        for t in range(ATT_G + 1):
            if t < ATT_G:
                score(t)
            if t >= 1:
                softmax(t - 1)
                value(t - 1)

    ones_rows = jnp.ones((BF16_ROWS, ATT_TK), BF16)
    acc_ref[...] = jnp.zeros_like(acc_ref)
    m_run = lax.fori_loop(0, i, step, tuple(jnp.full((1, m2), -jnp.inf, F32) for _ in range(ATT_G)))
    diagonal_step(m_run)

    lam = (jnp.exp(jnp.sum(lq1_ref[...] * lk1_ref[...])) - jnp.exp(jnp.sum(lq2_ref[...] * lk2_ref[...]))
           + lambda_init)
    for g in range(ATT_G):
        ot = acc_ref[g, :DIFF_V_DIM, :] / acc_ref[g, DIFF_V_DIM:DIFF_V_DIM + 1, :]
        o = (ot[:, :ATT_TQ] - lam * ot[:, ATT_TQ:]).T
        o = _rms(o, gout_ref[...]) * (1.0 - lambda_init)
        o_ref[0, :, g * DIFF_V_DIM:(g + 1) * DIFF_V_DIM] = o.astype(BF16)


def _diff_attn(qt, k, vt, lq1, lk1, lq2, lk2, g_out, lambda_init):
    bsz, t, _ = k.shape
    gw = ATT_G * DIFF_V_DIM
    lam_spec = _resident((1, DIFF_HEAD_DIM))
    return pl.pallas_call(
        functools.partial(_diff_attn_body, lambda_init=lambda_init),
        grid=(bsz, DIFF_HEADS // ATT_G, t // ATT_TQ),
        in_specs=[pl.BlockSpec((1, gw, ATT_TQ), lambda b, h, i: (b, h, i)),
                  pl.BlockSpec((1, t, gw), lambda b, h, i: (b, 0, h)),
                  pl.BlockSpec((1, t // ATT_TK, gw, ATT_TK), lambda b, h, i: (b, 0, h, 0)),
                  lam_spec, lam_spec, lam_spec, lam_spec, _resident((1, DIFF_V_DIM))],
        out_specs=pl.BlockSpec((1, ATT_TQ, gw), lambda b, h, i: (b, i, h)),
        out_shape=jax.ShapeDtypeStruct(k.shape, BF16),
        scratch_shapes=[pltpu.VMEM((ATT_G, DIFF_V_DIM + BF16_ROWS, 2 * ATT_TQ), F32)],
        compiler_params=_params(("arbitrary", "arbitrary", "arbitrary")),
        name="diff_attn",
    )(qt, k, vt, lq1.reshape(1, -1), lk1.reshape(1, -1), lq2.reshape(1, -1), lk2.reshape(1, -1),
      g_out.reshape(1, -1))


def kernel(x, l0_norm_ffn1, l0_ffn1_w_in, l0_ffn1_w_down, l0_norm_mix, l0_gla_w_in, l0_gla_w_gate2, l0_gla_b_gate2, l0_gla_norm_out, l0_gla_w_out, l0_norm_ffn2, l0_ffn2_w_in, l0_ffn2_w_down, l1_norm_ffn1, l1_ffn1_w_in, l1_ffn1_w_down, l1_norm_mix, l1_diff_w_in, l1_diff_lambda_q1, l1_diff_lambda_k1, l1_diff_lambda_q2, l1_diff_lambda_k2, l1_diff_norm_out, l1_diff_w_out, l1_norm_ffn2, l1_ffn2_w_in, l1_ffn2_w_down, final_norm):
    bsz, t, d = x.shape
    m = bsz * t
    lambda_init = 0.8 - 0.6 * math.exp(-0.3 * 1)

    h = _ffn(x.reshape(m, d), l0_norm_ffn1, l0_ffn1_w_in, l0_ffn1_w_down, final_norm, False)
    h = _gla_layer(h, l0_norm_mix, l0_gla_w_in, l0_gla_w_gate2, l0_gla_b_gate2,
                   l0_gla_norm_out, l0_gla_w_out, t)
    h = _ffn(h, l0_norm_ffn2, l0_ffn2_w_in, l0_ffn2_w_down, final_norm, False)

    h = _ffn(h, l1_norm_ffn1, l1_ffn1_w_in, l1_ffn1_w_down, final_norm, False)
    qt, k, vt = _qkv_proj(h.reshape(bsz, t, d), l1_norm_mix, l1_diff_w_in)
    o = _diff_attn(qt, k, vt, l1_diff_lambda_q1, l1_diff_lambda_k1, l1_diff_lambda_q2, l1_diff_lambda_k2,
                   l1_diff_norm_out, lambda_init)
    h = _ffn(h, l1_norm_ffn2, l1_ffn2_w_in, l1_ffn2_w_down, final_norm, True,
             mixer_out=o.reshape(m, d), w_out=l1_diff_w_out)
    return h.reshape(bsz, t, d)
```

```python
import functools
import math

import jax
import jax.numpy as jnp
from jax import lax
from jax.experimental import pallas as pl
from jax.experimental.pallas import tpu as pltpu

F32 = jnp.float32
BF16 = jnp.bfloat16

D_MODEL = 1024
D_FF = 2816
NORM_EPS = 1e-6

GLA_HEADS = 4
GLA_DK = 512
GLA_DV = 1024
GLA_HEAD_K = 128
GLA_HEAD_V = 256
GLA_GATE_RANK = 16
GLA_TAU = 16.0
GLA_CHUNK = 64
GLA_MAIN = 2 * GLA_DK + 2 * GLA_DV

DIFF_HEADS = 8
DIFF_HEAD_DIM = 64
DIFF_V_DIM = 128

LOG2_E = math.log2(math.e)
LANES = 128
BF16_ROWS = 16
VMEM_LIMIT = 56 * 1024 * 1024

FFN_TM = 1024
FFN_TM_FUSED = 512
FFN_TF = 256
GLA_PIECE = 512
GLA_TM = 512
PROJ_TM = 1024
ATT_TQ = 512
ATT_TK = 512
ATT_G = 4

NT_DIMS = (((1,), (1,)), ((), ()))
TN_DIMS = (((0,), (0,)), ((), ()))


def _rms(x, g):
    ms = jnp.mean(x * x, axis=-1, keepdims=True)
    return (x * lax.rsqrt(ms + NORM_EPS)) * g


def _silu(x):
    return x * (1.0 / (1.0 + jnp.exp(-x)))


def _resident(shape):
    nd = len(shape)
    return pl.BlockSpec(shape, lambda *_: (0,) * nd, pipeline_mode=pl.Buffered(1))


def _params(sem):
    return pltpu.CompilerParams(dimension_semantics=sem, vmem_limit_bytes=VMEM_LIMIT)


def _ffn_body(*refs, final_norm, fused_proj):
    if fused_proj:
        x_ref, o_ref, wout_ref, g_ref, win_ref, wdown_ref, gfin_ref, y_ref = refs
        x = x_ref[...] + jnp.dot(o_ref[...], wout_ref[...].astype(BF16), preferred_element_type=F32)
    else:
        x_ref, g_ref, win_ref, wdown_ref, gfin_ref, y_ref = refs
        x = x_ref[...]
    xn = _rms(x, g_ref[...]).astype(BF16)
    acc = None
    for c in range(D_FF // FFN_TF):
        lo = c * FFN_TF
        w_gate = win_ref[:, lo:lo + FFN_TF].astype(BF16)
        w_up = win_ref[:, D_FF + lo:D_FF + lo + FFN_TF].astype(BF16)
        gate = jnp.dot(xn, w_gate, preferred_element_type=F32)
        up = jnp.dot(xn, w_up, preferred_element_type=F32)
        h = (_silu(gate) * up).astype(BF16)
        part = jnp.dot(h, wdown_ref[lo:lo + FFN_TF, :].astype(BF16), preferred_element_type=F32)
        acc = part if acc is None else acc + part
    y = x + 0.5 * acc
    if final_norm:
        y = _rms(y, gfin_ref[...])
    y_ref[...] = y


def _ffn(x2d, g, w_in, w_down, g_final, final_norm, mixer_out=None, w_out=None):
    m = x2d.shape[0]
    fused = mixer_out is not None
    tm = FFN_TM_FUSED if fused else FFN_TM
    row = pl.BlockSpec((tm, D_MODEL), lambda i: (i, 0))
    args, specs = [x2d], [row]
    if fused:
        args += [mixer_out, w_out]
        specs += [row, _resident((D_MODEL, D_MODEL))]
    args += [g.reshape(1, -1), w_in, w_down, g_final.reshape(1, -1)]
    specs += [_resident((1, D_MODEL)), _resident((D_MODEL, 2 * D_FF)), _resident((D_FF, D_MODEL)),
              _resident((1, D_MODEL))]
    return pl.pallas_call(
        functools.partial(_ffn_body, final_norm=final_norm, fused_proj=fused),
        grid=(m // tm,),
        in_specs=specs,
        out_specs=row,
        out_shape=jax.ShapeDtypeStruct((m, D_MODEL), F32),
        compiler_params=_params(("arbitrary",)),
        name="ffn",
    )(*args)


def _gla_body(xn_ref, xp_ref, gmix_ref, win_ref, wg1_ref, wg2_ref, bg2_ref, gout_ref, wout_ref,
              o_ref, proj_ref, la_ref, state_ref, og_ref, *, tiles_per_seq):
    g = pl.program_id(0)
    nxt = g % 2
    prv = 1 - nxt

    @pl.when(g == 0)
    def _():
        proj_ref[...] = jnp.zeros_like(proj_ref)
        la_ref[...] = jnp.zeros_like(la_ref)

    @pl.when((g == 0) | ((g - 1) % tiles_per_seq == 0))
    def _():
        state_ref[...] = jnp.zeros_like(state_ref)

    hn = _rms(xn_ref[...], gmix_ref[...]).astype(BF16)

    rows = lax.broadcasted_iota(jnp.int32, (GLA_CHUNK, GLA_CHUNK), 0)
    cols = lax.broadcasted_iota(jnp.int32, (GLA_CHUNK, GLA_CHUNK), 1)
    causal = rows >= cols
    tril = causal.astype(BF16)
    scale = GLA_HEAD_K ** -0.5
    n_chunks = GLA_TM // GLA_CHUNK

    def project(piece):
        if piece < GLA_MAIN // GLA_PIECE:
            c0 = piece * GLA_PIECE
            proj_ref[nxt, :, c0:c0 + GLA_PIECE] = jnp.dot(hn, win_ref[:, c0:c0 + GLA_PIECE],
                                                          preferred_element_type=F32)
        else:
            g_lr = jnp.dot(hn, wg1_ref[...], preferred_element_type=F32)
            z = jnp.dot(g_lr.astype(BF16), wg2_ref[...], preferred_element_type=F32) + bg2_ref[...]
            la_ref[nxt] = (jnp.minimum(z, 0.0) - jnp.log(1.0 + jnp.exp(-jnp.abs(z)))) * (1.0 / GLA_TAU)

    def rows_of(n):
        return slice(n * GLA_CHUNK, (n + 1) * GLA_CHUNK)

    def ks(h):
        return slice(h * GLA_HEAD_K, (h + 1) * GLA_HEAD_K)

    def vs(h):
        return slice(h * GLA_HEAD_V, (h + 1) * GLA_HEAD_V)

    chunk_heads = [(n, h) for n in range(n_chunks) for h in range(GLA_HEADS)]
    project(0)
    cum = []
    for n in range(n_chunks):
        la = la_ref[prv, rows_of(n), :]
        la_hi = la.astype(BF16)
        la_lo = (la - la_hi.astype(F32)).astype(BF16)
        cum.append(jnp.dot(tril, la_hi, preferred_element_type=F32)
                   + jnp.dot(tril, la_lo, preferred_element_type=F32))

    project(1)
    q_dec, k_inv, k_tail, decay, v = [], [], [], [], []
    for n in range(n_chunks):
        b = cum[n]
        b_last = b[GLA_CHUNK - 1:GLA_CHUNK]
        q = proj_ref[prv, rows_of(n), 0:GLA_DK]
        k = proj_ref[prv, rows_of(n), GLA_DK:2 * GLA_DK]
        q_dec.append(((q * scale) * jnp.exp(b)).astype(BF16))
        k_inv.append((k * jnp.exp(-b)).astype(BF16))
        k_tail.append((k * jnp.exp(b_last - b)).astype(BF16))
        decay.append(jnp.exp(b_last))
        v.append(proj_ref[prv, rows_of(n), 2 * GLA_DK:2 * GLA_DK + GLA_DV].astype(BF16))

    project(2)
    att = {}
    for n, h in chunk_heads:
        a = lax.dot_general(q_dec[n][:, ks(h)], k_inv[n][:, ks(h)], NT_DIMS, preferred_element_type=F32)
        att[n, h] = jnp.where(causal, a, 0.0).astype(BF16)

    project(3)
    upd = {}
    for n, h in chunk_heads:
        upd[n, h] = lax.dot_general(v[n][:, vs(h)], k_tail[n][:, ks(h)], TN_DIMS,
                                    preferred_element_type=F32)

    project(4)
    start = {}
    for h in range(GLA_HEADS):
        s_t = state_ref[h]
        for n in range(n_chunks):
            start[n, h] = s_t.astype(BF16)
            s_t = s_t * decay[n][:, ks(h)] + upd[n, h]
        state_ref[h] = s_t

    project(5)
    outs = {}
    for n, h in chunk_heads:
        o = jnp.dot(att[n, h], v[n][:, vs(h)], preferred_element_type=F32)
        outs[n, h] = o + lax.dot_general(q_dec[n][:, ks(h)], start[n, h], NT_DIMS, preferred_element_type=F32)

    project(6)
    for n, h in chunk_heads:
        r0 = 2 * GLA_DK + GLA_DV + h * GLA_HEAD_V
        r = proj_ref[prv, rows_of(n), r0:r0 + GLA_HEAD_V]
        o = _rms(outs[n, h], gout_ref[...])
        og_ref[rows_of(n), vs(h)] = (o * _silu(r)).astype(BF16)

    o_ref[...] = xp_ref[...] + jnp.dot(og_ref[...], wout_ref[...], preferred_element_type=F32)


def _gla_layer(x2d, g_mix, w_in, w_gate2, b_gate2, g_out, w_out, seq_len):
    m = x2d.shape[0]
    n_tiles = m // GLA_TM
    w_main = w_in[:, :GLA_MAIN].astype(BF16)
    w_g1 = jnp.pad(w_in[:, GLA_MAIN:], ((0, 0), (0, LANES - GLA_GATE_RANK))).astype(BF16)
    w_g2 = jnp.pad(w_gate2, ((0, LANES - GLA_GATE_RANK), (0, 0))).astype(BF16)
    nxt_spec = pl.BlockSpec((GLA_TM, D_MODEL), lambda g: (jnp.minimum(g, n_tiles - 1), 0))
    prv_spec = pl.BlockSpec((GLA_TM, D_MODEL), lambda g: (jnp.maximum(g - 1, 0), 0))
    return pl.pallas_call(
        functools.partial(_gla_body, tiles_per_seq=seq_len // GLA_TM),
        grid=(n_tiles + 1,),
        in_specs=[nxt_spec, prv_spec, _resident((1, D_MODEL)), _resident((D_MODEL, GLA_MAIN)),
                  _resident((D_MODEL, LANES)), _resident((LANES, GLA_DK)), _resident((1, GLA_DK)),
                  _resident((1, GLA_HEAD_V)), _resident((GLA_DV, D_MODEL))],
        out_specs=prv_spec,
        out_shape=jax.ShapeDtypeStruct((m, D_MODEL), F32),
        scratch_shapes=[pltpu.VMEM((2, GLA_TM, GLA_MAIN), F32),
                        pltpu.VMEM((2, GLA_TM, GLA_DK), F32),
                        pltpu.VMEM((GLA_HEADS, GLA_HEAD_V, GLA_HEAD_K), F32),
                        pltpu.VMEM((GLA_TM, GLA_DV), BF16)],
        compiler_params=_params(("arbitrary",)),
        name="gla_layer",
    )(x2d, x2d, g_mix.reshape(1, -1), w_main, w_g1, w_g2, b_gate2.reshape(1, -1),
      g_out.reshape(1, -1), w_out.astype(BF16))


def _qkv_body(x_ref, g_ref, wqt_ref, wk_ref, wvt_ref, qt_ref, k_ref, vt_ref):
    hn = _rms(x_ref[0], g_ref[...]).astype(BF16)
    qt = lax.dot_general(wqt_ref[...], hn, NT_DIMS, preferred_element_type=F32)
    qt_ref[0] = (qt * (DIFF_HEAD_DIM ** -0.5 * LOG2_E)).astype(BF16)
    k_ref[0] = jnp.dot(hn, wk_ref[...], preferred_element_type=F32).astype(BF16)
    vt = lax.dot_general(wvt_ref[...], hn, NT_DIMS, preferred_element_type=F32)
    for c in range(PROJ_TM // ATT_TK):
        vt_ref[0, c] = vt[:, c * ATT_TK:(c + 1) * ATT_TK].astype(BF16)


def _qkv_proj(x, g, w_in):
    bsz, t, _ = x.shape
    w_q, w_k, w_v = jnp.split(w_in.astype(BF16), 3, axis=1)
    return pl.pallas_call(
        _qkv_body,
        grid=(bsz, t // PROJ_TM),
        in_specs=[pl.BlockSpec((1, PROJ_TM, D_MODEL), lambda b, i: (b, i, 0)),
                  _resident((1, D_MODEL)), _resident((D_MODEL, D_MODEL)),
                  _resident((D_MODEL, D_MODEL)), _resident((D_MODEL, D_MODEL))],
        out_specs=[pl.BlockSpec((1, D_MODEL, PROJ_TM), lambda b, i: (b, 0, i)),
                   pl.BlockSpec((1, PROJ_TM, D_MODEL), lambda b, i: (b, i, 0)),
                   pl.BlockSpec((1, PROJ_TM // ATT_TK, D_MODEL, ATT_TK), lambda b, i: (b, i, 0, 0))],
        out_shape=[jax.ShapeDtypeStruct((bsz, D_MODEL, t), BF16),
                   jax.ShapeDtypeStruct((bsz, t, D_MODEL), BF16),
                   jax.ShapeDtypeStruct((bsz, t // ATT_TK, D_MODEL, ATT_TK), BF16)],
        compiler_params=_params(("arbitrary", "arbitrary")),
        name="diff_qkv",
    )(x, g.reshape(1, -1), w_q.T, w_k, w_v.T)


def _diff_attn_body(qt_ref, k_ref, vt_ref, lq1_ref, lk1_ref, lq2_ref, lk2_ref, gout_ref, o_ref, acc_ref,
                    *, lambda_init):
    i = pl.program_id(2)
    m2 = 2 * ATT_TQ
    feat = lax.broadcasted_iota(jnp.int32, (DIFF_V_DIM, ATT_TQ), 0)
    q2t = []
    for g in range(ATT_G):
        qt = qt_ref[0, g * DIFF_V_DIM:(g + 1) * DIFF_V_DIM, :].astype(F32)
        q2t.append(jnp.concatenate([jnp.where(feat < DIFF_HEAD_DIM, qt, 0.0),
                                    jnp.where(feat >= DIFF_HEAD_DIM, qt, 0.0)], axis=1).astype(BF16))

    heads = [slice(g * DIFF_V_DIM, (g + 1) * DIFF_V_DIM) for g in range(ATT_G)]

    def step(j, m_run, diag_offset=None):
        scores, probs = {}, {}

        def score(g):
            k = k_ref[0, pl.ds(j * ATT_TK, ATT_TK), heads[g]]
            s = jnp.dot(k, q2t[g], preferred_element_type=F32)
            if diag_offset is not None:
                kpos = lax.broadcasted_iota(jnp.int32, s.shape, 0) + diag_offset
                qpos = lax.broadcasted_iota(jnp.int32, s.shape, 1) % ATT_TQ
                s = jnp.where(qpos >= kpos, s, -jnp.inf)
            scores[g] = s

        def softmax(g):
            m_new = jnp.maximum(m_run[g], jnp.max(scores[g], axis=0, keepdims=True))
            probs[g] = (m_new, jnp.exp2(m_run[g] - m_new), jnp.exp2(scores[g] - m_new).astype(BF16))

        def value(g):
            _, alpha, p = probs[g]
            vt1 = jnp.concatenate([vt_ref[0, j, heads[g], :], ones_rows], axis=0)
            acc_ref[g] = alpha * acc_ref[g] + jnp.dot(vt1, p, preferred_element_type=F32)

        for t in range(ATT_G + 1):
            if t < ATT_G:
                score(t)
            if t >= 1:
                softmax(t - 1)
                value(t - 1)
        return tuple(probs[g][0] for g in range(ATT_G))

    half = ATT_TK // 2

    def later_half(x):
        return jnp.concatenate([x[:, half:ATT_TQ], x[:, ATT_TQ + half:]], axis=1)

    def diagonal_step(m_run):
        scores, probs = {}, {}

        def score(g):
            k_lo = k_ref[0, pl.ds(i * ATT_TK, half), heads[g]]
            k_hi = k_ref[0, pl.ds(i * ATT_TK + half, half), heads[g]]
            s_lo = jnp.dot(k_lo, q2t[g], preferred_element_type=F32)
            s_hi = jnp.dot(k_hi, later_half(q2t[g]), preferred_element_type=F32)
            kpos = lax.broadcasted_iota(jnp.int32, s_lo.shape, 0)
            qpos = lax.broadcasted_iota(jnp.int32, s_lo.shape, 1) % ATT_TQ
            s_lo = jnp.where(qpos >= kpos, s_lo, -jnp.inf)
            kpos = lax.broadcasted_iota(jnp.int32, s_hi.shape, 0)
            qpos = lax.broadcasted_iota(jnp.int32, s_hi.shape, 1) % half
            s_hi = jnp.where(qpos >= kpos, s_hi, -jnp.inf)
            scores[g] = (s_lo, s_hi)

        def softmax(g):
            s_lo, s_hi = scores[g]
            m_lo = jnp.max(s_lo, axis=0, keepdims=True)
            m_hi = jnp.max(s_hi, axis=0, keepdims=True)
            m_blk = jnp.concatenate([m_lo[:, :half], jnp.maximum(m_lo[:, half:ATT_TQ], m_hi[:, :half]),
                                     m_lo[:, ATT_TQ:ATT_TQ + half],
                                     jnp.maximum(m_lo[:, ATT_TQ + half:], m_hi[:, half:])], axis=1)
            m_new = jnp.maximum(m_run[g], m_blk)
            probs[g] = (jnp.exp2(m_run[g] - m_new), jnp.exp2(s_lo - m_new).astype(BF16),
                        jnp.exp2(s_hi - later_half(m_new)).astype(BF16))

        def value(g):
            alpha, p_lo, p_hi = probs[g]
            vt1 = jnp.concatenate([vt_ref[0, i, heads[g], :], ones_rows], axis=0)
            lo = alpha * acc_ref[g] + jnp.dot(vt1[:, :half], p_lo, preferred_element_type=F32)
            hi = jnp.dot(vt1[:, half:], p_hi, preferred_element_type=F32)
            acc_ref[g] = jnp.concatenate([lo[:, :half], lo[:, half:ATT_TQ] + hi[:, :half],
                                          lo[:, ATT_TQ:ATT_TQ + half], lo[:, ATT_TQ + half:] + hi[:, half:]],
                                         axis=1)

        for t in range(ATT_G + 1):
            if t < ATT_G:
                score(t)
            if t >= 1:
                softmax(t - 1)
                value(t - 1)

    ones_rows = jnp.ones((BF16_ROWS, ATT_TK), BF16)
    acc_ref[...] = jnp.zeros_like(acc_ref)
    m_run = lax.fori_loop(0, i, step, tuple(jnp.full((1, m2), -jnp.inf, F32) for _ in range(ATT_G)))
    diagonal_step(m_run)

    lam = (jnp.exp(jnp.sum(lq1_ref[...] * lk1_ref[...])) - jnp.exp(jnp.sum(lq2_ref[...] * lk2_ref[...]))
           + lambda_init)
    for g in range(ATT_G):
        inv_l = 1.0 / acc_ref[g, DIFF_V_DIM:DIFF_V_DIM + 1, :]
        ot = acc_ref[g, :DIFF_V_DIM, :] * inv_l
        o = (ot[:, :ATT_TQ] - lam * ot[:, ATT_TQ:]).T
        o = _rms(o, gout_ref[...]) * (1.0 - lambda_init)
        o_ref[0, :, g * DIFF_V_DIM:(g + 1) * DIFF_V_DIM] = o.astype(BF16)


def _diff_attn(qt, k, vt, lq1, lk1, lq2, lk2, g_out, lambda_init):
    bsz, t, _ = k.shape
    gw = ATT_G * DIFF_V_DIM
    lam_spec = _resident((1, DIFF_HEAD_DIM))
    return pl.pallas_call(
        functools.partial(_diff_attn_body, lambda_init=lambda_init),
        grid=(bsz, DIFF_HEADS // ATT_G, t // ATT_TQ),
        in_specs=[pl.BlockSpec((1, gw, ATT_TQ), lambda b, h, i: (b, h, i)),
                  pl.BlockSpec((1, t, gw), lambda b, h, i: (b, 0, h)),
                  pl.BlockSpec((1, t // ATT_TK, gw, ATT_TK), lambda b, h, i: (b, 0, h, 0)),
                  lam_spec, lam_spec, lam_spec, lam_spec, _resident((1, DIFF_V_DIM))],
        out_specs=pl.BlockSpec((1, ATT_TQ, gw), lambda b, h, i: (b, i, h)),
        out_shape=jax.ShapeDtypeStruct(k.shape, BF16),
        scratch_shapes=[pltpu.VMEM((ATT_G, DIFF_V_DIM + BF16_ROWS, 2 * ATT_TQ), F32)],
        compiler_params=_params(("arbitrary", "arbitrary", "arbitrary")),
        name="diff_attn",
    )(qt, k, vt, lq1.reshape(1, -1), lk1.reshape(1, -1), lq2.reshape(1, -1), lk2.reshape(1, -1),
      g_out.reshape(1, -1))


def kernel(x, l0_norm_ffn1, l0_ffn1_w_in, l0_ffn1_w_down, l0_norm_mix, l0_gla_w_in, l0_gla_w_gate2, l0_gla_b_gate2, l0_gla_norm_out, l0_gla_w_out, l0_norm_ffn2, l0_ffn2_w_in, l0_ffn2_w_down, l1_norm_ffn1, l1_ffn1_w_in, l1_ffn1_w_down, l1_norm_mix, l1_diff_w_in, l1_diff_lambda_q1, l1_diff_lambda_k1, l1_diff_lambda_q2, l1_diff_lambda_k2, l1_diff_norm_out, l1_diff_w_out, l1_norm_ffn2, l1_ffn2_w_in, l1_ffn2_w_down, final_norm):
    bsz, t, d = x.shape
    m = bsz * t
    lambda_init = 0.8 - 0.6 * math.exp(-0.3 * 1)

    h = _ffn(x.reshape(m, d), l0_norm_ffn1, l0_ffn1_w_in, l0_ffn1_w_down, final_norm, False)
    h = _gla_layer(h, l0_norm_mix, l0_gla_w_in, l0_gla_w_gate2, l0_gla_b_gate2,
                   l0_gla_norm_out, l0_gla_w_out, t)
    h = _ffn(h, l0_norm_ffn2, l0_ffn2_w_in, l0_ffn2_w_down, final_norm, False)

    h = _ffn(h, l1_norm_ffn1, l1_ffn1_w_in, l1_ffn1_w_down, final_norm, False)
    qt, k, vt = _qkv_proj(h.reshape(bsz, t, d), l1_norm_mix, l1_diff_w_in)
    o = _diff_attn(qt, k, vt, l1_diff_lambda_q1, l1_diff_lambda_k1, l1_diff_lambda_q2, l1_diff_lambda_k2,
                   l1_diff_norm_out, lambda_init)
    h = _ffn(h, l1_norm_ffn2, l1_ffn2_w_in, l1_ffn2_w_down, final_norm, True,
             mixer_out=o.reshape(m, d), w_out=l1_diff_w_out)
    return h.reshape(bsz, t, d)
```
